```python
import math
import jax
import jax.numpy as jnp
from jax import lax
import numpy as np

D_MODEL = 1024
BATCH = 4
SEQ = 8192
DEPTH = 2

D_MIX = D_MODEL
GROUP_W = D_MIX // 4
HEAD_DIM = 64
GLA_HEADS = GROUP_W // HEAD_DIM
GLA_GATE_RANK = 16
GLA_GATE_NORMALIZER = 16.0
GLA_CHUNK = 64
FOX_HEADS = GROUP_W // HEAD_DIM
FOX_BLOCK = 128
SSM_HEADS = GROUP_W // HEAD_DIM
SSM_HEAD_DIM = HEAD_DIM
SSM_GROUPS = 2
SSM_STATE = 128
SSM_CONV = 4
SSM_CHUNK = 128
SSM_XBC = GROUP_W + 2 * SSM_GROUPS * SSM_STATE
SC_GROUPS = 4
SC_CONV = 3
D_FF = 2816
EPS = 1e-6

GLA_COLS = 4 * GROUP_W + GLA_GATE_RANK
FOX_COLS = 3 * GROUP_W + FOX_HEADS
SSM_COLS = GROUP_W + SSM_XBC + SSM_HEADS
SC_COLS = 3 * GROUP_W
IN_COLS = GLA_COLS + FOX_COLS + SSM_COLS + SC_COLS

kernel_name = 'hybrid_parallel_heads_gla_fox_ssd_shortconv'


def rms_norm(x, g):
    xf = x.astype(jnp.float32)
    y = xf * lax.rsqrt(jnp.mean(xf * xf, axis=-1, keepdims=True) + EPS)
    return (y * g.astype(jnp.float32)).astype(x.dtype)


def headwise_rms(x, g, n):
    shp = x.shape
    y = rms_norm(x.reshape(shp[:-1] + (shp[-1] // n, n)), g.reshape(-1, n))
    return y.reshape(shp)


def swiglu(h, w_gate, w_up, w_down):
    return (jax.nn.silu(h @ w_gate) * (h @ w_up)) @ w_down


def causal_depthwise_conv(u, w):
    K, C = w.shape
    return lax.conv_general_dilated(
        u, w[:, None, :].astype(u.dtype), window_strides=(1,),
        padding=[(K - 1, 0)], dimension_numbers=('NWC', 'WIO', 'NWC'),
        feature_group_count=C)


def gla_mixer(q, k, v, g_out, g_lr, w_gate_up, b_gate, norm_g):
    dtype = q.dtype
    f32 = jnp.float32
    Bsz, L, _ = q.shape
    H, D, C = GLA_HEADS, HEAD_DIM, GLA_CHUNK
    NC = L // C
    log_a = jax.nn.log_sigmoid((g_lr @ w_gate_up + b_gate).astype(f32)) / GLA_GATE_NORMALIZER

    def chunked(t):
        return t.astype(f32).reshape(Bsz, NC, C, H, D)

    qc = chunked(q) * (D ** -0.5)
    kc, vc, gc = chunked(k), chunked(v), chunked(log_a)
    b = jnp.cumsum(gc, axis=2)
    b_last = b[:, :, -1]
    q_dec = qc * jnp.exp(b)
    k_dec = kc * jnp.exp(-b)
    causal = jnp.tril(jnp.ones((C, C), bool))
    att = jnp.einsum('bnihd,bnjhd->bnhij', q_dec, k_dec)
    att = jnp.where(causal, att, 0.0)
    o_intra = jnp.einsum('bnhij,bnjhv->bnihv', att, vc)
    k_to_end = kc * jnp.exp(b_last[:, :, None] - b)
    chunk_kv = jnp.einsum('bnjhd,bnjhv->nbhdv', k_to_end, vc)
    chunk_decay = jnp.exp(b_last).transpose(1, 0, 2, 3)

    def step(S, inp):
        kv, dec = inp
        return S * dec[..., None] + kv, S

    S0 = jnp.zeros((Bsz, H, D, D), f32)
    _, S_prev = lax.scan(step, S0, (chunk_kv, chunk_decay))
    o_inter = jnp.einsum('bnihd,nbhdv->bnihv', q_dec, S_prev)
    o = (o_intra + o_inter).reshape(Bsz, L, H * D)
    o = headwise_rms(o, norm_g, D) * jax.nn.silu(g_out.astype(f32))
    return o.astype(dtype)


def fox_mixer(q, k, v, f_logit, b_forget, q_norm, k_norm, out_norm):
    dtype = q.dtype
    f32 = jnp.float32
    Bsz, L, _ = q.shape
    H, D, T = FOX_HEADS, HEAD_DIM, FOX_BLOCK
    NB = L // T
    qh = rms_norm(q.astype(f32).reshape(Bsz, L, H, D), q_norm) * (D ** -0.5)
    kh = rms_norm(k.astype(f32).reshape(Bsz, L, H, D), k_norm)
    vh = v.astype(f32).reshape(Bsz, L, H, D)
    log_f = jax.nn.log_sigmoid(f_logit.astype(f32) + b_forget.astype(f32))
    F = jnp.cumsum(log_f, axis=1)
    q_blocks = qh.reshape(Bsz, NB, T, H, D).transpose(1, 0, 2, 3, 4)
    F_blocks = F.reshape(Bsz, NB, T, H).transpose(1, 0, 2, 3)
    F_keys = F.transpose(0, 2, 1)[:, :, None, :]
    k_pos = jnp.arange(L)

    def block(args):
        qb, Fb, i = args
        s = jnp.einsum('bqhd,bkhd->bhqk', qb, kh)
        s = s + (Fb.transpose(0, 2, 1)[..., None] - F_keys)
        q_pos = i * T + jnp.arange(T)
        s = jnp.where(k_pos[None, :] <= q_pos[:, None], s, -jnp.inf)
        p = jax.nn.softmax(s, axis=-1)
        return jnp.einsum('bhqk,bkhd->bqhd', p, vh)

    o = lax.map(block, (q_blocks, F_blocks, jnp.arange(NB)))
    o = o.transpose(1, 0, 2, 3, 4).reshape(Bsz, L, H * D)
    return headwise_rms(o, out_norm, D).astype(dtype)


def ssd_mixer(z, xbc, dt_raw, conv_w, conv_b, dt_bias, A_log, D_skip, norm_g):
    dtype = z.dtype
    f32 = jnp.float32
    Bsz, L, _ = z.shape
    H, P, G, N, Q = SSM_HEADS, SSM_HEAD_DIM, SSM_GROUPS, SSM_STATE, SSM_CHUNK
    NC = L // Q
    xbc = jax.nn.silu(causal_depthwise_conv(xbc, conv_w) + conv_b).astype(f32)
    xs = xbc[..., :GROUP_W].reshape(Bsz, L, H, P)
    Bm = jnp.repeat(xbc[..., GROUP_W:GROUP_W + G * N].reshape(Bsz, L, G, N), H // G, axis=2)
    Cm = jnp.repeat(xbc[..., GROUP_W + G * N:].reshape(Bsz, L, G, N), H // G, axis=2)
    dt = jax.nn.softplus(dt_raw.astype(f32) + dt_bias.astype(f32))
    A = -jnp.exp(A_log.astype(f32))
    a = (dt * A).reshape(Bsz, NC, Q, H).transpose(0, 3, 1, 2)
    Xd = (xs * dt[..., None]).reshape(Bsz, NC, Q, H, P)
    Bc = Bm.reshape(Bsz, NC, Q, H, N)
    Cc = Cm.reshape(Bsz, NC, Q, H, N)
    a_cs = jnp.cumsum(a, axis=-1)
    causal = jnp.tril(jnp.ones((Q, Q), bool))
    seg = a_cs[..., :, None] - a_cs[..., None, :]
    Lmat = jnp.exp(jnp.where(causal, seg, -jnp.inf))
    scores = jnp.einsum('bcihn,bcjhn->bhcij', Cc, Bc) * Lmat
    y_diag = jnp.einsum('bhcij,bcjhp->bcihp', scores, Xd)
    decay_to_end = jnp.exp(a_cs[..., -1:] - a_cs)
    chunk_states = jnp.einsum('bcjhn,bhcj,bcjhp->cbhpn', Bc, decay_to_end, Xd)
    chunk_decay = jnp.exp(a_cs[..., -1]).transpose(2, 0, 1)

    def step(S, inp):
        st, dec = inp
        return S * dec[..., None, None] + st, S

    S0 = jnp.zeros((Bsz, H, P, N), f32)
    _, S_prev = lax.scan(step, S0, (chunk_states, chunk_decay))
    y_off = jnp.einsum('bcihn,cbhpn,bhci->bcihp', Cc, S_prev, jnp.exp(a_cs))
    y = (y_diag + y_off).reshape(Bsz, L, H, P) + xs * D_skip.astype(f32)[:, None]
    y = y.reshape(Bsz, L, H * P) * jax.nn.silu(z.astype(f32))
    return headwise_rms(y, norm_g, GROUP_W // G).astype(dtype)


def short_conv_mixer(b_gate, c_gate, val, conv_w, out_norm):
    y = b_gate * causal_depthwise_conv(c_gate * val, conv_w)
    return headwise_rms(y, out_norm, GROUP_W // SC_GROUPS)


def hybrid_mixing(h, w_in, gla_w_gate_up, gla_b_gate, gla_norm,
                  fox_b_forget, fox_q_norm, fox_k_norm, fox_out_norm,
                  ssm_conv_w, ssm_conv_b, ssm_dt_bias, ssm_A_log, ssm_D, ssm_norm,
                  sc_conv_w, sc_out_norm, w_out):
    proj = h @ w_in
    sizes = ([GROUP_W] * 4 + [GLA_GATE_RANK] + [GROUP_W] * 3 + [FOX_HEADS]
             + [GROUP_W, SSM_XBC, SSM_HEADS] + [GROUP_W] * 3)
    cuts = [int(c) for c in np.cumsum(sizes)[:-1]]
    (a_q, a_k, a_v, a_g, a_lr, b_q, b_k, b_v, b_f,
     c_z, c_xbc, c_dt, d_b, d_c, d_v) = jnp.split(proj, cuts, axis=-1)
    y_a = gla_mixer(a_q, a_k, a_v, a_g, a_lr, gla_w_gate_up, gla_b_gate, gla_norm)
    y_b = fox_mixer(b_q, b_k, b_v, b_f, fox_b_forget, fox_q_norm, fox_k_norm, fox_out_norm)
    y_c = ssd_mixer(c_z, c_xbc, c_dt, ssm_conv_w, ssm_conv_b, ssm_dt_bias, ssm_A_log, ssm_D, ssm_norm)
    y_d = short_conv_mixer(d_b, d_c, d_v, sc_conv_w, sc_out_norm)
    y = jnp.concatenate([y_a, y_b, y_c, y_d], axis=-1).astype(h.dtype)
    return y @ w_out


def setup_inputs(seed: int = 0) -> dict:
    key = jax.random.key(seed)
    k = jax.random.split(key, 27)
    f32 = jnp.float32
    Ld = DEPTH

    def nrm(kk, shape, scale):
        return jax.random.normal(kk, shape, f32) * scale

    def gain(kk, shape):
        return 1.0 + 0.02 * jax.random.normal(kk, shape, f32)

    dt0 = jnp.exp(jax.random.uniform(k[16], (Ld, SSM_HEADS), f32, math.log(1e-3), math.log(1e-1)))
    return {
        'x': nrm(k[0], (BATCH, SEQ, D_MODEL), 1.0),
        'ffn1_norm': gain(k[1], (Ld, D_MODEL)),
        'ffn1_w_gate': nrm(k[2], (Ld, D_MODEL, D_FF), D_MODEL ** -0.5),
        'ffn1_w_up': nrm(k[3], (Ld, D_MODEL, D_FF), D_MODEL ** -0.5),
        'ffn1_w_down': nrm(k[4], (Ld, D_FF, D_MODEL), D_FF ** -0.5),
        'mix_norm': gain(k[5], (Ld, D_MODEL)),
        'w_in': nrm(k[6], (Ld, D_MODEL, IN_COLS), D_MODEL ** -0.5),
        'gla_w_gate_up': nrm(k[7], (Ld, GLA_GATE_RANK, GROUP_W), GLA_GATE_RANK ** -0.5),
        'gla_b_gate': nrm(k[8], (Ld, GROUP_W), 0.1),
        'gla_norm': gain(k[9], (Ld, GROUP_W)),
        'fox_b_forget': 2.0 + nrm(k[10], (Ld, FOX_HEADS), 0.1),
        'fox_q_norm': gain(k[11], (Ld, HEAD_DIM)),
        'fox_k_norm': gain(k[12], (Ld, HEAD_DIM)),
        'fox_out_norm': gain(k[13], (Ld, GROUP_W)),
        'ssm_conv_w': nrm(k[14], (Ld, SSM_CONV, SSM_XBC), SSM_CONV ** -0.5),
        'ssm_conv_b': nrm(k[15], (Ld, SSM_XBC), 0.02),
        'ssm_dt_bias': dt0 + jnp.log(-jnp.expm1(-dt0)),
        'ssm_A_log': jnp.log(jax.random.uniform(k[17], (Ld, SSM_HEADS), f32, 1.0, 16.0)),
        'ssm_D': 1.0 + nrm(k[18], (Ld, SSM_HEADS), 0.1),
        'ssm_norm': gain(k[19], (Ld, GROUP_W)),
        'sc_conv_w': nrm(k[20], (Ld, SC_CONV, GROUP_W), SC_CONV ** -0.5),
        'sc_out_norm': gain(k[21], (Ld, GROUP_W)),
        'w_out': nrm(k[22], (Ld, D_MIX, D_MODEL), D_MIX ** -0.5),
        'ffn2_norm': gain(k[23], (Ld, D_MODEL)),
        'ffn2_w_gate': nrm(k[24], (Ld, D_MODEL, D_FF), D_MODEL ** -0.5),
        'ffn2_w_up': nrm(k[25], (Ld, D_MODEL, D_FF), D_MODEL ** -0.5),
        'ffn2_w_down': nrm(k[26], (Ld, D_FF, D_MODEL), D_FF ** -0.5),
    }


def reference(x, ffn1_norm, ffn1_w_gate, ffn1_w_up, ffn1_w_down,
              mix_norm, w_in, gla_w_gate_up, gla_b_gate, gla_norm,
              fox_b_forget, fox_q_norm, fox_k_norm, fox_out_norm,
              ssm_conv_w, ssm_conv_b, ssm_dt_bias, ssm_A_log, ssm_D, ssm_norm,
              sc_conv_w, sc_out_norm, w_out,
              ffn2_norm, ffn2_w_gate, ffn2_w_up, ffn2_w_down):
    for l in range(DEPTH):
        x = x + 0.5 * swiglu(rms_norm(x, ffn1_norm[l]), ffn1_w_gate[l], ffn1_w_up[l], ffn1_w_down[l])
        h = rms_norm(x, mix_norm[l])
        x = x + hybrid_mixing(h, w_in[l], gla_w_gate_up[l], gla_b_gate[l], gla_norm[l],
                              fox_b_forget[l], fox_q_norm[l], fox_k_norm[l], fox_out_norm[l],
                              ssm_conv_w[l], ssm_conv_b[l], ssm_dt_bias[l], ssm_A_log[l],
                              ssm_D[l], ssm_norm[l], sc_conv_w[l], sc_out_norm[l], w_out[l])
        x = x + 0.5 * swiglu(rms_norm(x, ffn2_norm[l]), ffn2_w_gate[l], ffn2_w_up[l], ffn2_w_down[l])
    return x
```

```python
import functools

import numpy as np
import jax
import jax.numpy as jnp
from jax import lax
from jax.experimental import pallas as pl
from jax.experimental.pallas import tpu as pltpu

F32, BF16 = jnp.float32, jnp.bfloat16

D_MODEL = 1024
GROUP_W = 256
HEAD_DIM = 64
N_HEADS = GROUP_W // HEAD_DIM
GLA_GATE_RANK = 16
GLA_GATE_NORMALIZER = 16.0
GLA_CHUNK = 64
SSM_GROUPS = 2
SSM_STATE = 128
SSM_CONV = 4
SSM_CHUNK = 128
SSM_XBC = GROUP_W + 2 * SSM_GROUPS * SSM_STATE
SC_GROUPS = 4
SC_CONV = 3
D_FF = 2816
EPS = 1e-6

GLA_COLS = 4 * GROUP_W + GLA_GATE_RANK
FOX_COLS = 3 * GROUP_W + N_HEADS
SSM_COLS = GROUP_W + SSM_XBC + N_HEADS
MAIN_COLS = 4 * GROUP_W + 3 * GROUP_W + (GROUP_W + SSM_XBC) + 3 * GROUP_W

LANES = 128
SMALL_LR = 0
SMALL_F = 16
SMALL_DT = 24
SUBLANES = 8

FFN_TM = 512
FF_CHUNK = 256
PROJ_TM = 512
MIX_T = 512
SC_T = 1024
ATT_TQ = 256
ATT_TK = 512
OUT_TM = 1024
HALO = 8

VMEM_LIMIT = 56 * 1024 * 1024


def _const_spec(shape):
    nd = len(shape)
    return pl.BlockSpec(shape, lambda *_: (0,) * nd, pipeline_mode=pl.Buffered(1))


def _params(n_axes):
    return pltpu.CompilerParams(dimension_semantics=("arbitrary",) * n_axes,
                                vmem_limit_bytes=VMEM_LIMIT)


def _split_bf16(x, pieces):
    out, r = [], x
    for i in range(pieces):
        p = r.astype(BF16)
        out.append(p)
        if i + 1 < pieces:
            r = r - p.astype(F32)
    return out


def _dot_x01(x, m01, pieces):
    acc = None
    for p in _split_bf16(x, pieces):
        d = jnp.dot(p, m01, preferred_element_type=F32)
        acc = d if acc is None else acc + d
    return acc


def _dot_01x(m01, x, pieces):
    acc = None
    for p in _split_bf16(x, pieces):
        d = jnp.dot(m01, p, preferred_element_type=F32)
        acc = d if acc is None else acc + d
    return acc


def _dot_nt(a, b):
    return lax.dot_general(a, b, (((1,), (1,)), ((), ())), preferred_element_type=F32)


def _dot_tn(a, b):
    return lax.dot_general(a, b, (((0,), (0,)), ((), ())), preferred_element_type=F32)


def _log_sigmoid(x):
    return jnp.minimum(x, 0.0) - jnp.log1p(jnp.exp(-jnp.abs(x)))


def _softplus(x):
    return jnp.maximum(x, 0.0) + jnp.log1p(jnp.exp(-jnp.abs(x)))


def _silu(x):
    return x * jax.nn.sigmoid(x)


def _rms_rows(x, gain):
    ms = jnp.mean(x * x, axis=-1, keepdims=True)
    return x * lax.rsqrt(ms + EPS) * gain


def _head_id(shape, axis, width):
    return lax.broadcasted_iota(jnp.int32, shape, axis) // width


def _ffn_body(x_ref, g_ref, wg_ref, wu_ref, wd_ref, o_ref, a_scr):
    x = x_ref[...]
    h = _rms_rows(x, g_ref[...]).astype(BF16)
    for j in range(D_FF // FF_CHUNK):
        sl = slice(j * FF_CHUNK, (j + 1) * FF_CHUNK)
        gate = jnp.dot(h, wg_ref[:, sl], preferred_element_type=F32)
        up = jnp.dot(h, wu_ref[:, sl], preferred_element_type=F32)
        a_scr[:, sl] = (_silu(gate) * up).astype(BF16)
    y = jnp.dot(a_scr[...], wd_ref[...], preferred_element_type=F32)
    o_ref[...] = x + 0.5 * y


def _ffn(x, gain, wg, wu, wd):
    m = x.shape[0]
    row = lambda i: (i, 0)
    return pl.pallas_call(
        _ffn_body,
        grid=(m // FFN_TM,),
        in_specs=[pl.BlockSpec((FFN_TM, D_MODEL), row),
                  _const_spec((1, D_MODEL)),
                  _const_spec((D_MODEL, D_FF)),
                  _const_spec((D_MODEL, D_FF)),
                  _const_spec((D_FF, D_MODEL))],
        out_specs=pl.BlockSpec((FFN_TM, D_MODEL), row),
        out_shape=jax.ShapeDtypeStruct((m, D_MODEL), F32),
        scratch_shapes=[pltpu.VMEM((FFN_TM, D_FF), BF16)],
        compiler_params=_params(1),
        name="ffn",
    )(x, gain, wg, wu, wd)


_PROJ_SEGMENTS = (4 * GROUP_W, 3 * GROUP_W, GROUP_W + SSM_XBC, 3 * GROUP_W)


def _inproj_body(x_ref, g_ref, wm_ref, ws_ref,
                 gla_ref, fox_ref, ssd_ref, sc_ref, small_ref, small_t_ref):
    h = _rms_rows(x_ref[...], g_ref[...]).astype(BF16)
    col = 0
    for out_ref, width in zip((gla_ref, fox_ref, ssd_ref, sc_ref), _PROJ_SEGMENTS):
        for c in range(0, width, GROUP_W):
            out_ref[:, c:c + GROUP_W] = jnp.dot(
                h, wm_ref[:, col + c:col + c + GROUP_W], preferred_element_type=F32).astype(BF16)
        col += width
    small = jnp.dot(h, ws_ref[...], preferred_element_type=F32)
    small_ref[...] = small
    small_t_ref[...] = small.T


def _inproj(x, gain, w_main, w_small):
    m = x.shape[0]
    row = lambda i: (i, 0)
    outs = [jax.ShapeDtypeStruct((m, w), BF16) for w in _PROJ_SEGMENTS]
    outs += [jax.ShapeDtypeStruct((m, LANES), F32), jax.ShapeDtypeStruct((LANES, m), F32)]
    out_specs = [pl.BlockSpec((PROJ_TM, w), row) for w in _PROJ_SEGMENTS]
    out_specs += [pl.BlockSpec((PROJ_TM, LANES), row),
                  pl.BlockSpec((LANES, PROJ_TM), lambda i: (0, i))]
    return pl.pallas_call(
        _inproj_body,
        grid=(m // PROJ_TM,),
        in_specs=[pl.BlockSpec((PROJ_TM, D_MODEL), row),
                  _const_spec((1, D_MODEL)),
                  _const_spec((D_MODEL, MAIN_COLS)),
                  _const_spec((D_MODEL, LANES))],
        out_specs=out_specs,
        out_shape=outs,
        compiler_params=_params(1),
        name="inproj",
    )(x, gain, w_main, w_small)


def _gla_body(x_ref, sm_ref, wup_ref, bg_ref, ng_ref, lb_ref, e_ref, o_ref, st_scr, o_scr):
    t = MIX_T

    @pl.when(pl.program_id(1) == 0)
    def _():
        st_scr[...] = jnp.zeros_like(st_scr)

    q = x_ref[:, 0:GROUP_W].astype(F32) * (HEAD_DIM ** -0.5)
    k = x_ref[:, GROUP_W:2 * GROUP_W].astype(F32)
    v = x_ref[:, 2 * GROUP_W:3 * GROUP_W]
    g_out = x_ref[:, 3 * GROUP_W:4 * GROUP_W].astype(F32)

    z = jnp.dot(sm_ref[...].astype(BF16), wup_ref[...], preferred_element_type=F32) + bg_ref[...]
    log_a = _log_sigmoid(z) * (1.0 / GLA_GATE_NORMALIZER)
    b = _dot_01x(lb_ref[...], log_a, 3)

    c = GLA_CHUNK
    head_rows = _head_id((GROUP_W, GROUP_W), 0, HEAD_DIM)
    head_cols = _head_id((GROUP_W, GROUP_W), 1, HEAD_DIM)
    block_diag = head_rows == head_cols
    lane_head = _head_id((c, GROUP_W), 1, HEAD_DIM)
    row_id = lax.broadcasted_iota(jnp.int32, (c, GROUP_W), 0)
    key_id = lax.broadcasted_iota(jnp.int32, (c, GROUP_W), 1) % HEAD_DIM
    causal = key_id <= row_id

    for ci in range(t // c):
        r = slice(ci * c, (ci + 1) * c)
        bc = b[r]
        bl = bc[c - 1:c, :]
        qd = (q[r] * jnp.exp(bc)).astype(BF16)
        kd = (k[r] * jnp.exp(-bc)).astype(BF16)
        ke = (k[r] * jnp.exp(bl - bc)).astype(BF16)
        vb = v[r]
        zero = jnp.zeros_like(kd)
        k_stack = jnp.concatenate([jnp.where(lane_head == h, kd, zero) for h in range(N_HEADS)], axis=0)
        v_stack = jnp.concatenate([jnp.where(lane_head == h, vb, zero) for h in range(N_HEADS)], axis=0)
        att = _dot_nt(qd, k_stack)
        att = jnp.where(causal, att, 0.0).astype(BF16)
        st = st_scr[...]
        o_scr[r, :] = (jnp.dot(att, v_stack, preferred_element_type=F32)
                       + _dot_nt(qd, st.astype(BF16)))
        kv_t = _dot_tn(vb, ke)
        st_scr[...] = st * jnp.exp(bl) + jnp.where(block_diag, kv_t, 0.0)

    o = o_scr[...]
    ssq = _dot_x01(o * o, e_ref[...], 2)
    y = o * lax.rsqrt(ssq * (1.0 / HEAD_DIM) + EPS) * ng_ref[...] * _silu(g_out)
    o_ref[...] = y.astype(BF16)


def _gla(gla_in, small, wup_pad, b_gate, norm_g, lb, e64, batch, seq):
    nt = seq // MIX_T
    row = lambda b, t: (b * nt + t, 0)
    return pl.pallas_call(
        _gla_body,
        grid=(batch, nt),
        in_specs=[pl.BlockSpec((MIX_T, 4 * GROUP_W), row),
                  pl.BlockSpec((MIX_T, LANES), row),
                  _const_spec((LANES, GROUP_W)),
                  _const_spec((1, GROUP_W)),
                  _const_spec((1, GROUP_W)),
                  _const_spec((MIX_T, MIX_T)),
                  _const_spec((GROUP_W, GROUP_W))],
        out_specs=pl.BlockSpec((MIX_T, GROUP_W), row),
        out_shape=jax.ShapeDtypeStruct((batch * seq, GROUP_W), BF16),
        scratch_shapes=[pltpu.VMEM((GROUP_W, GROUP_W), F32),
                        pltpu.VMEM((MIX_T, GROUP_W), F32)],
        compiler_params=_params(2),
        name="gla",
    )(gla_in, small, wup_pad, b_gate, norm_g, lb, e64)


def _foxprep_body(q_ref, k_ref, ft_ref, bf_ref, qg_ref, kg_ref, e_ref, u_ref,
                  qn_ref, kn_ref, f_ref, carry):
    @pl.when(pl.program_id(1) == 0)
    def _():
        carry[...] = jnp.zeros_like(carry)

    def head_norm(x, gain):
        ssq = _dot_x01(x * x, e_ref[...], 2)
        return x * lax.rsqrt(ssq * (1.0 / HEAD_DIM) + EPS) * gain

    qn_ref[...] = (head_norm(q_ref[...].astype(F32), qg_ref[...]) * (HEAD_DIM ** -0.5)).astype(BF16)
    kn_ref[...] = head_norm(k_ref[...].astype(F32), kg_ref[...]).astype(BF16)

    log_f = _log_sigmoid(ft_ref[...] + bf_ref[...])
    f_ref[...] = _dot_x01(log_f, u_ref[...], 3) + carry[:, 0:1]
    carry[...] = carry[...] + jnp.sum(log_f, axis=-1, keepdims=True)


def _foxprep(fox_in, small_t, b_forget, q_gain, k_gain, e64, utri, batch, seq):
    nt = seq // MIX_T
    m = batch * seq
    return pl.pallas_call(
        _foxprep_body,
        grid=(batch, nt),
        in_specs=[pl.BlockSpec((MIX_T, GROUP_W), lambda b, t: (b * nt + t, 0)),
                  pl.BlockSpec((MIX_T, GROUP_W), lambda b, t: (b * nt + t, 1)),
                  pl.BlockSpec((SUBLANES, MIX_T), lambda b, t: (SMALL_F // SUBLANES, b * nt + t)),
                  _const_spec((SUBLANES, 1)),
                  _const_spec((1, GROUP_W)),
                  _const_spec((1, GROUP_W)),
                  _const_spec((GROUP_W, GROUP_W)),
                  _const_spec((MIX_T, MIX_T))],
        out_specs=[pl.BlockSpec((MIX_T, GROUP_W), lambda b, t: (b * nt + t, 0)),
                   pl.BlockSpec((MIX_T, GROUP_W), lambda b, t: (b * nt + t, 0)),
                   pl.BlockSpec((SUBLANES, MIX_T), lambda b, t: (0, b * nt + t))],
        out_shape=[jax.ShapeDtypeStruct((m, GROUP_W), BF16),
                   jax.ShapeDtypeStruct((m, GROUP_W), BF16),
                   jax.ShapeDtypeStruct((SUBLANES, m), F32)],
        scratch_shapes=[pltpu.VMEM((SUBLANES, LANES), F32)],
        compiler_params=_params(2),
        name="foxprep",
    )(fox_in, fox_in, small_t, b_forget, q_gain, k_gain, e64, utri)


def _attn_body(q_ref, k_ref, v_ref, f_ref, og_ref, o_ref, qs_scr, m_scr, l_scr, acc_scr, p_scr):
    tq, tk = ATT_TQ, ATT_TK
    i = pl.program_id(1)
    q0 = pl.multiple_of(i * tq, tq)

    q = q_ref[...]
    lane_head = _head_id((tq, GROUP_W), 1, HEAD_DIM)
    for h in range(N_HEADS):
        qs_scr[h * tq:(h + 1) * tq, :] = jnp.where(lane_head == h, q, jnp.zeros_like(q))
    m_scr[...] = jnp.full_like(m_scr, -jnp.inf)
    l_scr[...] = jnp.zeros_like(l_scr)
    acc_scr[...] = jnp.zeros_like(acc_scr)

    f_here = f_ref[:, pl.ds(q0, LANES)][:, 0:1]

    def step(j, masked):
        c0 = pl.multiple_of(j * tk, tk)
        k = k_ref[pl.ds(c0, tk), :]
        v = v_ref[pl.ds(c0, tk), :]
        s = _dot_nt(qs_scr[...], k)
        bias = f_here - f_ref[:, pl.ds(c0, tk)]
        if masked:
            rows = lax.broadcasted_iota(jnp.int32, (tq, tk), 0) + q0
            cols = lax.broadcasted_iota(jnp.int32, (tq, tk), 1) + c0
            keep = cols <= rows
        for h in range(N_HEADS):
            rs = slice(h * tq, (h + 1) * tq)
            sh = s[rs] + bias[h:h + 1, :]
            if masked:
                sh = jnp.where(keep, sh, -jnp.inf)
            m_prev = m_scr[rs]
            m_new = jnp.maximum(m_prev, jnp.max(sh, axis=-1, keepdims=True))
            alpha = jnp.exp(m_prev - m_new)
            p = jnp.exp(sh - m_new)
            l_scr[rs] = alpha * l_scr[rs] + jnp.sum(p, axis=-1, keepdims=True)
            m_scr[rs] = m_new
            p_scr[rs, :] = p.astype(BF16)
            acc_scr[rs, :] = acc_scr[rs, :] * alpha
        acc_scr[...] += jnp.dot(p_scr[...], v, preferred_element_type=F32)

    n_full = (i * tq) // tk

    def loop_body(j, carry):
        step(j, False)
        return carry

    lax.fori_loop(0, n_full, loop_body, 0)
    step(n_full, True)

    y = jnp.zeros((tq, GROUP_W), F32)
    for h in range(N_HEADS):
        rs = slice(h * tq, (h + 1) * tq)
        oh = acc_scr[rs, :] / l_scr[rs]
        mine = lane_head == h
        ssq = jnp.sum(jnp.where(mine, oh * oh, 0.0), axis=-1, keepdims=True)
        y = jnp.where(mine, oh * lax.rsqrt(ssq * (1.0 / HEAD_DIM) + EPS), y)
    o_ref[...] = (y * og_ref[...]).astype(BF16)


def _attn(qn, kn, fox_in, f_cum, out_gain, batch, seq):
    nq = seq // ATT_TQ
    rows = N_HEADS * ATT_TQ
    return pl.pallas_call(
        _attn_body,
        grid=(batch, nq),
        in_specs=[pl.BlockSpec((ATT_TQ, GROUP_W), lambda b, i: (b * nq + i, 0)),
                  pl.BlockSpec((seq, GROUP_W), lambda b, i: (b, 0)),
                  pl.BlockSpec((seq, GROUP_W), lambda b, i: (b, 2)),
                  pl.BlockSpec((SUBLANES, seq), lambda b, i: (0, b)),
                  _const_spec((1, GROUP_W))],
        out_specs=pl.BlockSpec((ATT_TQ, GROUP_W), lambda b, i: (b * nq + i, 0)),
        out_shape=jax.ShapeDtypeStruct((batch * seq, GROUP_W), BF16),
        scratch_shapes=[pltpu.VMEM((rows, GROUP_W), BF16),
                        pltpu.VMEM((rows, 1), F32),
                        pltpu.VMEM((rows, 1), F32),
                        pltpu.VMEM((rows, GROUP_W), F32),
                        pltpu.VMEM((rows, ATT_TK), BF16)],
        compiler_params=_params(2),
        name="fox_attn",
    )(qn, kn, fox_in, f_cum, out_gain)


def _ssd_body(x_ref, sm_ref, smt_ref, cw_ref, cb_ref, dtbc_ref, ac_ref, dtbr_ref, ar_ref,
              dskip_ref, ng_ref, ltri_ref, utri_ref, o_ref, xb_scr, st_scr):
    t, qc, n = MIX_T, SSM_CHUNK, SSM_STATE

    @pl.when(pl.program_id(1) == 0)
    def _():
        xb_scr[0:HALO, :] = jnp.zeros((HALO, SSM_XBC), F32)
        st_scr[...] = jnp.zeros_like(st_scr)

    z = x_ref[:, 0:GROUP_W].astype(F32)
    xb_scr[HALO:HALO + t, :] = x_ref[:, GROUP_W:GROUP_W + SSM_XBC].astype(F32)
    conv = cb_ref[...]
    for kk in range(SSM_CONV):
        conv = conv + cw_ref[kk:kk + 1, :] * xb_scr[pl.ds(HALO - (SSM_CONV - 1) + kk, t), :]
    xb_scr[0:HALO, :] = xb_scr[t:t + HALO, :]
    xact = _silu(conv)
    xs = xact[:, 0:GROUP_W]

    dt_col = _softplus(sm_ref[...] + dtbc_ref[...])
    a_col = dt_col * ac_ref[...]
    a_row = _softplus(smt_ref[...] + dtbr_ref[...]) * ar_ref[...]

    lane_head = _head_id((qc, GROUP_W), 1, HEAD_DIM)
    tril = (lax.broadcasted_iota(jnp.int32, (qc, qc), 1)
            <= lax.broadcasted_iota(jnp.int32, (qc, qc), 0))

    for ci in range(t // qc):
        r = slice(ci * qc, (ci + 1) * qc)
        acs_col = _dot_01x(ltri_ref[...], a_col[r], 3)
        acs_row = _dot_x01(a_row[:, r], utri_ref[...], 3)
        b_all = xact[r, GROUP_W:GROUP_W + SSM_GROUPS * n]
        c_all = xact[r, GROUP_W + SSM_GROUPS * n:]
        cb = [_dot_nt(c_all[:, g * n:(g + 1) * n].astype(BF16), b_all[:, g * n:(g + 1) * n].astype(BF16))
              for g in range(SSM_GROUPS)]

        dt_lanes = jnp.zeros((qc, GROUP_W), F32)
        scores, c_scaled, b_decayed, e_last = [], [], [], []
        for h in range(N_HEADS):
            g = h // (N_HEADS // SSM_GROUPS)
            ac = acs_col[:, SMALL_DT + h:SMALL_DT + h + 1]
            ar = acs_row[h:h + 1, :]
            lmat = jnp.exp(jnp.where(tril, ac - ar, -jnp.inf))
            scores.append((cb[g] * lmat).astype(BF16))
            a_last = ac[qc - 1:qc, :]
            e_last.append(jnp.exp(a_last))
            b_decayed.append((b_all[:, g * n:(g + 1) * n] * jnp.exp(a_last - ac)).astype(BF16))
            c_scaled.append((c_all[:, g * n:(g + 1) * n] * jnp.exp(ac)).astype(BF16))
            dt_lanes = jnp.where(lane_head == h, dt_col[r, SMALL_DT + h:SMALL_DT + h + 1], dt_lanes)

        xd = (xs[r] * dt_lanes).astype(BF16)
        zero = jnp.zeros_like(xd)
        xd_heads = [jnp.where(lane_head == h, xd, zero) for h in range(N_HEADS)]
        st = st_scr[...]
        lhs = jnp.concatenate(scores + c_scaled, axis=1)
        rhs = jnp.concatenate(xd_heads + [st.astype(BF16)], axis=0)
        y = jnp.dot(lhs, rhs, preferred_element_type=F32)
        for h in range(N_HEADS):
            hs = slice(h * n, (h + 1) * n)
            st_scr[hs, :] = st[hs] * e_last[h] + _dot_tn(b_decayed[h], xd_heads[h])

        y = (y + xs[r] * dskip_ref[...]) * _silu(z[r])
        for g in range(SSM_GROUPS):
            gs = slice(g * LANES, (g + 1) * LANES)
            yg = y[:, gs]
            ms = jnp.mean(yg * yg, axis=-1, keepdims=True)
            o_ref[r, gs] = (yg * lax.rsqrt(ms + EPS) * ng_ref[:, gs]).astype(BF16)


def _ssd(ssd_in, small, small_t, conv_w, conv_b, dtb_col, a_col, dtb_row, a_row, d_skip, norm_g,
         ltri, utri, batch, seq):
    nt = seq // MIX_T
    row = lambda b, t: (b * nt + t, 0)
    return pl.pallas_call(
        _ssd_body,
        grid=(batch, nt),
        in_specs=[pl.BlockSpec((MIX_T, GROUP_W + SSM_XBC), row),
                  pl.BlockSpec((MIX_T, LANES), row),
                  pl.BlockSpec((SUBLANES, MIX_T), lambda b, t: (SMALL_DT // SUBLANES, b * nt + t)),
                  _const_spec((SSM_CONV, SSM_XBC)),
                  _const_spec((1, SSM_XBC)),
                  _const_spec((1, LANES)),
                  _const_spec((1, LANES)),
                  _const_spec((SUBLANES, 1)),
                  _const_spec((SUBLANES, 1)),
                  _const_spec((1, GROUP_W)),
                  _const_spec((1, GROUP_W)),
                  _const_spec((SSM_CHUNK, SSM_CHUNK)),
                  _const_spec((SSM_CHUNK, SSM_CHUNK))],
        out_specs=pl.BlockSpec((MIX_T, GROUP_W), row),
        out_shape=jax.ShapeDtypeStruct((batch * seq, GROUP_W), BF16),
        scratch_shapes=[pltpu.VMEM((MIX_T + HALO, SSM_XBC), F32),
                        pltpu.VMEM((N_HEADS * SSM_STATE, GROUP_W), F32)],
        compiler_params=_params(2),
        name="ssd",
    )(ssd_in, small, small_t, conv_w, conv_b, dtb_col, a_col, dtb_row, a_row, d_skip, norm_g, ltri, utri)


def _sc_body(x_ref, cw_ref, ng_ref, e_ref, o_ref, u_scr):
    t = SC_T

    @pl.when(pl.program_id(1) == 0)
    def _():
        u_scr[0:HALO, :] = jnp.zeros((HALO, GROUP_W), F32)

    b_gate = x_ref[:, 0:GROUP_W].astype(F32)
    c_gate = x_ref[:, GROUP_W:2 * GROUP_W].astype(F32)
    val = x_ref[:, 2 * GROUP_W:3 * GROUP_W].astype(F32)
    u_scr[HALO:HALO + t, :] = c_gate * val
    conv = jnp.zeros((t, GROUP_W), F32)
    for kk in range(SC_CONV):
        conv = conv + cw_ref[kk:kk + 1, :] * u_scr[pl.ds(HALO - (SC_CONV - 1) + kk, t), :]
    u_scr[0:HALO, :] = u_scr[t:t + HALO, :]
    y = b_gate * conv
    ssq = _dot_x01(y * y, e_ref[...], 2)
    o_ref[...] = (y * lax.rsqrt(ssq * (1.0 / (GROUP_W // SC_GROUPS)) + EPS) * ng_ref[...]).astype(BF16)


def _short_conv(sc_in, conv_w, norm_g, e64, batch, seq):
    nt = seq // SC_T
    row = lambda b, t: (b * nt + t, 0)
    return pl.pallas_call(
        _sc_body,
        grid=(batch, nt),
        in_specs=[pl.BlockSpec((SC_T, 3 * GROUP_W), row),
                  _const_spec((SC_CONV, GROUP_W)),
                  _const_spec((1, GROUP_W)),
                  _const_spec((GROUP_W, GROUP_W))],
        out_specs=pl.BlockSpec((SC_T, GROUP_W), row),
        out_shape=jax.ShapeDtypeStruct((batch * seq, GROUP_W), BF16),
        scratch_shapes=[pltpu.VMEM((SC_T + HALO, GROUP_W), F32)],
        compiler_params=_params(2),
        name="short_conv",
    )(sc_in, conv_w, norm_g, e64)


def _outproj_body(x_ref, ya_ref, yb_ref, yc_ref, yd_ref, w_ref, o_ref):
    acc = x_ref[...]
    for i, y_ref in enumerate((ya_ref, yb_ref, yc_ref, yd_ref)):
        acc = acc + jnp.dot(y_ref[...], w_ref[i * GROUP_W:(i + 1) * GROUP_W, :],
                            preferred_element_type=F32)
    o_ref[...] = acc


def _outproj(x, ya, yb, yc, yd, w_out):
    m = x.shape[0]
    row = lambda i: (i, 0)
    y_spec = pl.BlockSpec((OUT_TM, GROUP_W), row)
    return pl.pallas_call(
        _outproj_body,
        grid=(m // OUT_TM,),
        in_specs=[pl.BlockSpec((OUT_TM, D_MODEL), row), y_spec, y_spec, y_spec, y_spec,
                  _const_spec((4 * GROUP_W, D_MODEL))],
        out_specs=pl.BlockSpec((OUT_TM, D_MODEL), row),
        out_shape=jax.ShapeDtypeStruct((m, D_MODEL), F32),
        compiler_params=_params(1),
        name="outproj",
    )(x, ya, yb, yc, yd, w_out)


def _block_diag_ones(size, block):
    idx = np.arange(size) // block
    return jnp.asarray(idx[:, None] == idx[None, :], BF16)


def _chunk_lower_tri(size, block):
    i = np.arange(size)
    same = (i[:, None] // block) == (i[None, :] // block)
    return jnp.asarray(same & (i[None, :] <= i[:, None]), BF16)


def _row(v):
    return v.reshape(1, -1).astype(F32)


def _lane_slot(values, offset, width):
    return jnp.zeros((1, width), F32).at[0, offset:offset + values.shape[0]].set(values.astype(F32))


def _sublane_col(values):
    return jnp.zeros((SUBLANES, 1), F32).at[0:values.shape[0], 0].set(values.astype(F32))


def _regroup_w_in(w_in):
    gla_end = 4 * GROUP_W
    fox0 = GLA_COLS
    fox_end = fox0 + 3 * GROUP_W
    ssm0 = GLA_COLS + FOX_COLS
    ssm_end = ssm0 + GROUP_W + SSM_XBC
    sc0 = GLA_COLS + FOX_COLS + SSM_COLS
    w_main = jnp.concatenate([w_in[:, 0:gla_end], w_in[:, fox0:fox_end],
                              w_in[:, ssm0:ssm_end], w_in[:, sc0:]], axis=1).astype(BF16)
    w_small = jnp.zeros((D_MODEL, LANES), F32)
    w_small = w_small.at[:, SMALL_LR:SMALL_LR + GLA_GATE_RANK].set(w_in[:, gla_end:GLA_COLS])
    w_small = w_small.at[:, SMALL_F:SMALL_F + N_HEADS].set(w_in[:, fox_end:ssm0])
    w_small = w_small.at[:, SMALL_DT:SMALL_DT + N_HEADS].set(w_in[:, ssm_end:sc0])
    return w_main, w_small.astype(BF16)


def kernel(x, ffn1_norm, ffn1_w_gate, ffn1_w_up, ffn1_w_down, mix_norm, w_in, gla_w_gate_up, gla_b_gate, gla_norm, fox_b_forget, fox_q_norm, fox_k_norm, fox_out_norm, ssm_conv_w, ssm_conv_b, ssm_dt_bias, ssm_A_log, ssm_D, ssm_norm, sc_conv_w, sc_out_norm, w_out, ffn2_norm, ffn2_w_gate, ffn2_w_up, ffn2_w_down):
    batch, seq, _ = x.shape
    depth = w_in.shape[0]
    assert seq % MIX_T == 0 and seq % SC_T == 0 and seq % ATT_TK == 0
    assert (batch * seq) % OUT_TM == 0

    e64 = _block_diag_ones(GROUP_W, HEAD_DIM)
    lb_gla = _chunk_lower_tri(MIX_T, GLA_CHUNK)
    utri_t = jnp.asarray(np.triu(np.ones((MIX_T, MIX_T))), BF16)
    ltri_q = jnp.asarray(np.tril(np.ones((SSM_CHUNK, SSM_CHUNK))), BF16)
    utri_q = jnp.asarray(np.triu(np.ones((SSM_CHUNK, SSM_CHUNK))), BF16)

    xf = x.reshape(batch * seq, D_MODEL)
    for l in range(depth):
        xf = _ffn(xf, _row(ffn1_norm[l]), ffn1_w_gate[l].astype(BF16),
                  ffn1_w_up[l].astype(BF16), ffn1_w_down[l].astype(BF16))

        w_main, w_small = _regroup_w_in(w_in[l])
        gla_in, fox_in, ssd_in, sc_in, small, small_t = _inproj(xf, _row(mix_norm[l]), w_main, w_small)

        wup_pad = jnp.zeros((LANES, GROUP_W), F32).at[SMALL_LR:SMALL_LR + GLA_GATE_RANK].set(
            gla_w_gate_up[l]).astype(BF16)
        ya = _gla(gla_in, small, wup_pad, _row(gla_b_gate[l]), _row(gla_norm[l]), lb_gla, e64, batch, seq)

        qn, kn, f_cum = _foxprep(fox_in, small_t, _sublane_col(fox_b_forget[l]),
                                 _row(jnp.tile(fox_q_norm[l], N_HEADS)),
                                 _row(jnp.tile(fox_k_norm[l], N_HEADS)), e64, utri_t, batch, seq)
        yb = _attn(qn, kn, fox_in, f_cum, _row(fox_out_norm[l]), batch, seq)

        a_neg = -jnp.exp(ssm_A_log[l].astype(F32))
        yc = _ssd(ssd_in, small, small_t, ssm_conv_w[l].astype(F32), _row(ssm_conv_b[l]),
                  _lane_slot(ssm_dt_bias[l], SMALL_DT, LANES), _lane_slot(a_neg, SMALL_DT, LANES),
                  _sublane_col(ssm_dt_bias[l]), _sublane_col(a_neg),
                  _row(jnp.repeat(ssm_D[l], HEAD_DIM)), _row(ssm_norm[l]), ltri_q, utri_q, batch, seq)

        yd = _short_conv(sc_in, sc_conv_w[l].astype(F32), _row(sc_out_norm[l]), e64, batch, seq)

        xf = _outproj(xf, ya, yb, yc, yd, w_out[l].astype(BF16))
        xf = _ffn(xf, _row(ffn2_norm[l]), ffn2_w_gate[l].astype(BF16),
                  ffn2_w_up[l].astype(BF16), ffn2_w_down[l].astype(BF16))
    return xf.reshape(batch, seq, D_MODEL)
```

```python
import functools

import numpy as np
import jax
import jax.numpy as jnp
from jax import lax
from jax.experimental import pallas as pl
from jax.experimental.pallas import tpu as pltpu

F32, BF16 = jnp.float32, jnp.bfloat16

D_MODEL = 1024
GROUP_W = 256
HEAD_DIM = 64
N_HEADS = GROUP_W // HEAD_DIM
GLA_GATE_RANK = 16
GLA_GATE_NORMALIZER = 16.0
GLA_CHUNK = 64
SSM_GROUPS = 2
SSM_STATE = 128
SSM_CONV = 4
SSM_CHUNK = 128
SSM_XBC = GROUP_W + 2 * SSM_GROUPS * SSM_STATE
SC_GROUPS = 4
SC_CONV = 3
D_FF = 2816
EPS = 1e-6
LOG2_E = 1.4426950408889634

GLA_COLS = 4 * GROUP_W + GLA_GATE_RANK
FOX_COLS = 3 * GROUP_W + N_HEADS
SSM_COLS = GROUP_W + SSM_XBC + N_HEADS
MAIN_COLS = 4 * GROUP_W + 3 * GROUP_W + (GROUP_W + SSM_XBC) + 3 * GROUP_W

LANES = 128
SMALL_LR = 0
SMALL_F = 16
SMALL_DT = 24
SUBLANES = 8

FFN_TM = 512
FF_CHUNK = 256
PROJ_TM = 512
MIX_T = 512
SC_T = 1024
ATT_TQ = 512
ATT_TK = 512
OUT_TM = 1024
HALO = 8

VMEM_LIMIT = 56 * 1024 * 1024


def _const_spec(shape):
    nd = len(shape)
    return pl.BlockSpec(shape, lambda *_: (0,) * nd, pipeline_mode=pl.Buffered(1))


def _params(n_axes):
    return pltpu.CompilerParams(dimension_semantics=("arbitrary",) * n_axes,
                                vmem_limit_bytes=VMEM_LIMIT)


def _split_bf16(x, pieces):
    out, r = [], x
    for i in range(pieces):
        p = r.astype(BF16)
        out.append(p)
        if i + 1 < pieces:
            r = r - p.astype(F32)
    return out


def _dot_x01(x, m01, pieces):
    acc = None
    for p in _split_bf16(x, pieces):
        d = jnp.dot(p, m01, preferred_element_type=F32)
        acc = d if acc is None else acc + d
    return acc


def _dot_01x(m01, x, pieces):
    acc = None
    for p in _split_bf16(x, pieces):
        d = jnp.dot(m01, p, preferred_element_type=F32)
        acc = d if acc is None else acc + d
    return acc


def _dot_nt(a, b):
    return lax.dot_general(a, b, (((1,), (1,)), ((), ())), preferred_element_type=F32)


def _dot_tn(a, b):
    return lax.dot_general(a, b, (((0,), (0,)), ((), ())), preferred_element_type=F32)


def _log_sigmoid(x):
    return jnp.minimum(x, 0.0) - jnp.log1p(jnp.exp(-jnp.abs(x)))


def _softplus(x):
    return jnp.maximum(x, 0.0) + jnp.log1p(jnp.exp(-jnp.abs(x)))


def _silu(x):
    return x * jax.nn.sigmoid(x)


def _rms_rows(x, gain):
    ms = jnp.mean(x * x, axis=-1, keepdims=True)
    return x * lax.rsqrt(ms + EPS) * gain


def _head_id(shape, axis, width):
    return lax.broadcasted_iota(jnp.int32, shape, axis) // width


def _ffn_body(x_ref, g_ref, wg_ref, wu_ref, wd_ref, o_ref, a_scr):
    x = x_ref[...]
    h = _rms_rows(x, g_ref[...]).astype(BF16)
    for j in range(D_FF // FF_CHUNK):
        sl = slice(j * FF_CHUNK, (j + 1) * FF_CHUNK)
        gate = jnp.dot(h, wg_ref[:, sl], preferred_element_type=F32)
        up = jnp.dot(h, wu_ref[:, sl], preferred_element_type=F32)
        a_scr[:, sl] = (_silu(gate) * up).astype(BF16)
    y = jnp.dot(a_scr[...], wd_ref[...], preferred_element_type=F32)
    o_ref[...] = x + 0.5 * y


def _ffn(x, gain, wg, wu, wd):
    m = x.shape[0]
    row = lambda i: (i, 0)
    return pl.pallas_call(
        _ffn_body,
        grid=(m // FFN_TM,),
        in_specs=[pl.BlockSpec((FFN_TM, D_MODEL), row),
                  _const_spec((1, D_MODEL)),
                  _const_spec((D_MODEL, D_FF)),
                  _const_spec((D_MODEL, D_FF)),
                  _const_spec((D_FF, D_MODEL))],
        out_specs=pl.BlockSpec((FFN_TM, D_MODEL), row),
        out_shape=jax.ShapeDtypeStruct((m, D_MODEL), F32),
        scratch_shapes=[pltpu.VMEM((FFN_TM, D_FF), BF16)],
        compiler_params=_params(1),
        name="ffn",
    )(x, gain, wg, wu, wd)


_PROJ_SEGMENTS = (4 * GROUP_W, 3 * GROUP_W, GROUP_W + SSM_XBC, 3 * GROUP_W)


def _inproj_body(x_ref, g_ref, wm_ref, ws_ref,
                 gla_ref, fox_ref, ssd_ref, sc_ref, small_ref, small_t_ref):
    h = _rms_rows(x_ref[...], g_ref[...]).astype(BF16)
    col = 0
    for out_ref, width in zip((gla_ref, fox_ref, ssd_ref, sc_ref), _PROJ_SEGMENTS):
        for c in range(0, width, GROUP_W):
            out_ref[:, c:c + GROUP_W] = jnp.dot(
                h, wm_ref[:, col + c:col + c + GROUP_W], preferred_element_type=F32).astype(BF16)
        col += width
    small = jnp.dot(h, ws_ref[...], preferred_element_type=F32)
    small_ref[...] = small
    small_t_ref[...] = small.T


def _inproj(x, gain, w_main, w_small):
    m = x.shape[0]
    row = lambda i: (i, 0)
    outs = [jax.ShapeDtypeStruct((m, w), BF16) for w in _PROJ_SEGMENTS]
    outs += [jax.ShapeDtypeStruct((m, LANES), F32), jax.ShapeDtypeStruct((LANES, m), F32)]
    out_specs = [pl.BlockSpec((PROJ_TM, w), row) for w in _PROJ_SEGMENTS]
    out_specs += [pl.BlockSpec((PROJ_TM, LANES), row),
                  pl.BlockSpec((LANES, PROJ_TM), lambda i: (0, i))]
    return pl.pallas_call(
        _inproj_body,
        grid=(m // PROJ_TM,),
        in_specs=[pl.BlockSpec((PROJ_TM, D_MODEL), row),
                  _const_spec((1, D_MODEL)),
                  _const_spec((D_MODEL, MAIN_COLS)),
                  _const_spec((D_MODEL, LANES))],
        out_specs=out_specs,
        out_shape=outs,
        compiler_params=_params(1),
        name="inproj",
    )(x, gain, w_main, w_small)


def _gla_body(x_ref, sm_ref, wup_ref, bg_ref, ng_ref, lb_ref, e_ref, o_ref, st_scr, o_scr):
    t = MIX_T

    @pl.when(pl.program_id(1) == 0)
    def _():
        st_scr[...] = jnp.zeros_like(st_scr)

    q = x_ref[:, 0:GROUP_W].astype(F32) * (HEAD_DIM ** -0.5)
    k = x_ref[:, GROUP_W:2 * GROUP_W].astype(F32)
    v = x_ref[:, 2 * GROUP_W:3 * GROUP_W]
    g_out = x_ref[:, 3 * GROUP_W:4 * GROUP_W].astype(F32)

    z = jnp.dot(sm_ref[...].astype(BF16), wup_ref[...], preferred_element_type=F32) + bg_ref[...]
    log_a = _log_sigmoid(z) * (1.0 / GLA_GATE_NORMALIZER)
    b = _dot_01x(lb_ref[...], log_a, 3)

    c = GLA_CHUNK
    head_rows = _head_id((GROUP_W, GROUP_W), 0, HEAD_DIM)
    head_cols = _head_id((GROUP_W, GROUP_W), 1, HEAD_DIM)
    block_diag = head_rows == head_cols
    lane_head = _head_id((c, GROUP_W), 1, HEAD_DIM)
    row_id = lax.broadcasted_iota(jnp.int32, (c, GROUP_W), 0)
    key_id = lax.broadcasted_iota(jnp.int32, (c, GROUP_W), 1) % HEAD_DIM
    causal = key_id <= row_id

    for ci in range(t // c):
        r = slice(ci * c, (ci + 1) * c)
        bc = b[r]
        bl = bc[c - 1:c, :]
        qd = (q[r] * jnp.exp(bc)).astype(BF16)
        kd = (k[r] * jnp.exp(-bc)).astype(BF16)
        ke = (k[r] * jnp.exp(bl - bc)).astype(BF16)
        vb = v[r]
        zero = jnp.zeros_like(kd)
        k_stack = jnp.concatenate([jnp.where(lane_head == h, kd, zero) for h in range(N_HEADS)], axis=0)
        v_stack = jnp.concatenate([jnp.where(lane_head == h, vb, zero) for h in range(N_HEADS)], axis=0)
        att = _dot_nt(qd, k_stack)
        att = jnp.where(causal, att, 0.0).astype(BF16)
        st = st_scr[...]
        o_scr[r, :] = (jnp.dot(att, v_stack, preferred_element_type=F32)
                       + _dot_nt(qd, st.astype(BF16)))
        kv_t = _dot_tn(vb, ke)
        st_scr[...] = st * jnp.exp(bl) + jnp.where(block_diag, kv_t, 0.0)

    o = o_scr[...]
    ssq = _dot_x01(o * o, e_ref[...], 2)
    y = o * lax.rsqrt(ssq * (1.0 / HEAD_DIM) + EPS) * ng_ref[...] * _silu(g_out)
    o_ref[...] = y.astype(BF16)


def _gla(gla_in, small, wup_pad, b_gate, norm_g, lb, e64, batch, seq):
    nt = seq // MIX_T
    row = lambda b, t: (b * nt + t, 0)
    return pl.pallas_call(
        _gla_body,
        grid=(batch, nt),
        in_specs=[pl.BlockSpec((MIX_T, 4 * GROUP_W), row),
                  pl.BlockSpec((MIX_T, LANES), row),
                  _const_spec((LANES, GROUP_W)),
                  _const_spec((1, GROUP_W)),
                  _const_spec((1, GROUP_W)),
                  _const_spec((MIX_T, MIX_T)),
                  _const_spec((GROUP_W, GROUP_W))],
        out_specs=pl.BlockSpec((MIX_T, GROUP_W), row),
        out_shape=jax.ShapeDtypeStruct((batch * seq, GROUP_W), BF16),
        scratch_shapes=[pltpu.VMEM((GROUP_W, GROUP_W), F32),
                        pltpu.VMEM((MIX_T, GROUP_W), F32)],
        compiler_params=_params(2),
        name="gla",
    )(gla_in, small, wup_pad, b_gate, norm_g, lb, e64)


def _foxprep_body(q_ref, k_ref, ft_ref, bf_ref, qg_ref, kg_ref, e_ref, u_ref,
                  qn_ref, kn_ref, f_ref, carry):
    @pl.when(pl.program_id(1) == 0)
    def _():
        carry[...] = jnp.zeros_like(carry)

    def head_norm(x, gain):
        ssq = _dot_x01(x * x, e_ref[...], 2)
        return x * lax.rsqrt(ssq * (1.0 / HEAD_DIM) + EPS) * gain

    qn_ref[...] = (head_norm(q_ref[...].astype(F32), qg_ref[...]) * (HEAD_DIM ** -0.5 * LOG2_E)).astype(BF16)
    kn_ref[...] = head_norm(k_ref[...].astype(F32), kg_ref[...]).astype(BF16)

    log_f = _log_sigmoid(ft_ref[...] + bf_ref[...])
    f_ref[...] = _dot_x01(log_f, u_ref[...], 3) + carry[:, 0:1]
    carry[...] = carry[...] + jnp.sum(log_f, axis=-1, keepdims=True)


def _foxprep(fox_in, small_t, b_forget, q_gain, k_gain, e64, utri, batch, seq):
    nt = seq // MIX_T
    m = batch * seq
    return pl.pallas_call(
        _foxprep_body,
        grid=(batch, nt),
        in_specs=[pl.BlockSpec((MIX_T, GROUP_W), lambda b, t: (b * nt + t, 0)),
                  pl.BlockSpec((MIX_T, GROUP_W), lambda b, t: (b * nt + t, 1)),
                  pl.BlockSpec((SUBLANES, MIX_T), lambda b, t: (SMALL_F // SUBLANES, b * nt + t)),
                  _const_spec((SUBLANES, 1)),
                  _const_spec((1, GROUP_W)),
                  _const_spec((1, GROUP_W)),
                  _const_spec((GROUP_W, GROUP_W)),
                  _const_spec((MIX_T, MIX_T))],
        out_specs=[pl.BlockSpec((MIX_T, GROUP_W), lambda b, t: (b * nt + t, 0)),
                   pl.BlockSpec((MIX_T, GROUP_W), lambda b, t: (b * nt + t, 0)),
                   pl.BlockSpec((SUBLANES, MIX_T), lambda b, t: (0, b * nt + t))],
        out_shape=[jax.ShapeDtypeStruct((m, GROUP_W), BF16),
                   jax.ShapeDtypeStruct((m, GROUP_W), BF16),
                   jax.ShapeDtypeStruct((SUBLANES, m), F32)],
        scratch_shapes=[pltpu.VMEM((SUBLANES, LANES), F32)],
        compiler_params=_params(2),
        name="foxprep",
    )(fox_in, fox_in, small_t, b_forget, q_gain, k_gain, e64, utri)


def _attn_body(q_ref, k_ref, v_ref, f_ref, og_ref, o_ref, qs_scr, m_scr, acc_scr):
    tq, tk = ATT_TQ, ATT_TK
    i = pl.program_id(1)
    q0 = pl.multiple_of(i * tq, tq)

    q = q_ref[...]
    lane_head = _head_id((tq, GROUP_W), 1, HEAD_DIM)
    for h in range(N_HEADS):
        qs_scr[h * tq:(h + 1) * tq, :] = jnp.where(lane_head == h, q, jnp.zeros_like(q))
    m_scr[...] = jnp.full_like(m_scr, -jnp.inf)
    acc_scr[...] = jnp.zeros_like(acc_scr)

    f_here = f_ref[:, pl.ds(q0, LANES)][:, 0:1]
    key_lane_head = _head_id((tk, GROUP_W), 1, HEAD_DIM)

    def step(j, masked):
        c0 = pl.multiple_of(j * tk, tk)
        k = k_ref[pl.ds(c0, tk), :]
        v = v_ref[pl.ds(c0, tk), :]
        bias = (f_here - f_ref[:, pl.ds(c0, tk)]) * LOG2_E
        if masked:
            keep = (lax.broadcasted_iota(jnp.int32, (tq, tk), 1) + c0
                    <= lax.broadcasted_iota(jnp.int32, (tq, tk), 0) + q0)
        for h in range(N_HEADS):
            hs = slice(h * tq, (h + 1) * tq)
            s = _dot_nt(qs_scr[hs, :], k) + bias[h:h + 1, :]
            if masked:
                s = jnp.where(keep, s, -jnp.inf)
            m_prev = m_scr[hs]
            m_new = jnp.maximum(m_prev, jnp.max(s, axis=-1, keepdims=True))
            alpha = jnp.exp2(m_prev - m_new)
            p = jnp.exp2(s - jnp.tile(m_new, (1, tk // LANES))).astype(BF16)
            m_scr[hs] = m_new
            v_h = jnp.where(key_lane_head == (h + 1) % N_HEADS, jnp.ones_like(v), v)
            acc_scr[hs, :] = (acc_scr[hs, :] * jnp.tile(alpha, (1, GROUP_W // LANES))
                              + jnp.dot(p, v_h, preferred_element_type=F32))

    n_full = (i * tq) // tk

    def loop_body(j, carry):
        step(j, False)
        return carry

    lax.fori_loop(0, n_full, loop_body, 0)
    for jm in range(max(1, tq // tk)):
        step(n_full + jm, True)

    y = jnp.zeros((tq, GROUP_W), F32)
    for h in range(N_HEADS):
        acc = acc_scr[h * tq:(h + 1) * tq, :]
        row_sum = jnp.max(jnp.where(lane_head == (h + 1) % N_HEADS, acc, -jnp.inf), axis=-1, keepdims=True)
        oh = acc / row_sum
        mine = lane_head == h
        ssq = jnp.sum(jnp.where(mine, oh * oh, 0.0), axis=-1, keepdims=True)
        y = jnp.where(mine, oh * lax.rsqrt(ssq * (1.0 / HEAD_DIM) + EPS), y)
    o_ref[...] = (y * og_ref[...]).astype(BF16)


def _attn(qn, kn, fox_in, f_cum, out_gain, batch, seq):
    nq = seq // ATT_TQ
    rows = N_HEADS * ATT_TQ
    return pl.pallas_call(
        _attn_body,
        grid=(batch, nq),
        in_specs=[pl.BlockSpec((ATT_TQ, GROUP_W), lambda b, i: (b * nq + i, 0)),
                  pl.BlockSpec((seq, GROUP_W), lambda b, i: (b, 0)),
                  pl.BlockSpec((seq, GROUP_W), lambda b, i: (b, 2)),
                  pl.BlockSpec((SUBLANES, seq), lambda b, i: (0, b)),
                  _const_spec((1, GROUP_W))],
        out_specs=pl.BlockSpec((ATT_TQ, GROUP_W), lambda b, i: (b * nq + i, 0)),
        out_shape=jax.ShapeDtypeStruct((batch * seq, GROUP_W), BF16),
        scratch_shapes=[pltpu.VMEM((rows, GROUP_W), BF16),
                        pltpu.VMEM((rows, LANES), F32),
                        pltpu.VMEM((rows, GROUP_W), F32)],
        compiler_params=_params(2),
        name="fox_attn",
    )(qn, kn, fox_in, f_cum, out_gain)


def _ssd_body(x_ref, sm_ref, smt_ref, cw_ref, cb_ref, dtbc_ref, ac_ref, dtbr_ref, ar_ref,
              dskip_ref, ng_ref, ltri_ref, utri_ref, o_ref, xb_scr, st_scr):
    t, qc, n = MIX_T, SSM_CHUNK, SSM_STATE

    @pl.when(pl.program_id(1) == 0)
    def _():
        xb_scr[0:HALO, :] = jnp.zeros((HALO, SSM_XBC), F32)
        st_scr[...] = jnp.zeros_like(st_scr)

    z = x_ref[:, 0:GROUP_W].astype(F32)
    xb_scr[HALO:HALO + t, :] = x_ref[:, GROUP_W:GROUP_W + SSM_XBC].astype(F32)
    conv = cb_ref[...]
    for kk in range(SSM_CONV):
        conv = conv + cw_ref[kk:kk + 1, :] * xb_scr[pl.ds(HALO - (SSM_CONV - 1) + kk, t), :]
    xb_scr[0:HALO, :] = xb_scr[t:t + HALO, :]
    xact = _silu(conv)
    xs = xact[:, 0:GROUP_W]

    dt_col = _softplus(sm_ref[...] + dtbc_ref[...])
    a_col = dt_col * ac_ref[...]
    a_row = _softplus(smt_ref[...] + dtbr_ref[...]) * ar_ref[...]

    lane_head = _head_id((qc, GROUP_W), 1, HEAD_DIM)
    tril = (lax.broadcasted_iota(jnp.int32, (qc, qc), 1)
            <= lax.broadcasted_iota(jnp.int32, (qc, qc), 0))

    for ci in range(t // qc):
        r = slice(ci * qc, (ci + 1) * qc)
        acs_col = _dot_01x(ltri_ref[...], a_col[r], 3)
        acs_row = _dot_x01(a_row[:, r], utri_ref[...], 3)
        b_all = xact[r, GROUP_W:GROUP_W + SSM_GROUPS * n]
        c_all = xact[r, GROUP_W + SSM_GROUPS * n:]
        cb = [_dot_nt(c_all[:, g * n:(g + 1) * n].astype(BF16), b_all[:, g * n:(g + 1) * n].astype(BF16))
              for g in range(SSM_GROUPS)]

        dt_lanes = jnp.zeros((qc, GROUP_W), F32)
        scores, c_scaled, b_decayed, e_last = [], [], [], []
        for h in range(N_HEADS):
            g = h // (N_HEADS // SSM_GROUPS)
            ac = acs_col[:, SMALL_DT + h:SMALL_DT + h + 1]
            ar = acs_row[h:h + 1, :]
            lmat = jnp.exp(jnp.where(tril, ac - ar, -jnp.inf))
            scores.append((cb[g] * lmat).astype(BF16))
            a_last = ac[qc - 1:qc, :]
            e_last.append(jnp.exp(a_last))
            b_decayed.append((b_all[:, g * n:(g + 1) * n] * jnp.exp(a_last - ac)).astype(BF16))
            c_scaled.append((c_all[:, g * n:(g + 1) * n] * jnp.exp(ac)).astype(BF16))
            dt_lanes = jnp.where(lane_head == h, dt_col[r, SMALL_DT + h:SMALL_DT + h + 1], dt_lanes)

        xd = (xs[r] * dt_lanes).astype(BF16)
        zero = jnp.zeros_like(xd)
        xd_heads = [jnp.where(lane_head == h, xd, zero) for h in range(N_HEADS)]
        st = st_scr[...]
        lhs = jnp.concatenate(scores + c_scaled, axis=1)
        rhs = jnp.concatenate(xd_heads + [st.astype(BF16)], axis=0)
        y = jnp.dot(lhs, rhs, preferred_element_type=F32)
        for h in range(N_HEADS):
            hs = slice(h * n, (h + 1) * n)
            st_scr[hs, :] = st[hs] * e_last[h] + _dot_tn(b_decayed[h], xd_heads[h])

        y = (y + xs[r] * dskip_ref[...]) * _silu(z[r])
        for g in range(SSM_GROUPS):
            gs = slice(g * LANES, (g + 1) * LANES)
            yg = y[:, gs]
            ms = jnp.mean(yg * yg, axis=-1, keepdims=True)
            o_ref[r, gs] = (yg * lax.rsqrt(ms + EPS) * ng_ref[:, gs]).astype(BF16)


def _ssd(ssd_in, small, small_t, conv_w, conv_b, dtb_col, a_col, dtb_row, a_row, d_skip, norm_g,
         ltri, utri, batch, seq):
    nt = seq // MIX_T
    row = lambda b, t: (b * nt + t, 0)
    return pl.pallas_call(
        _ssd_body,
        grid=(batch, nt),
        in_specs=[pl.BlockSpec((MIX_T, GROUP_W + SSM_XBC), row),
                  pl.BlockSpec((MIX_T, LANES), row),
                  pl.BlockSpec((SUBLANES, MIX_T), lambda b, t: (SMALL_DT // SUBLANES, b * nt + t)),
                  _const_spec((SSM_CONV, SSM_XBC)),
                  _const_spec((1, SSM_XBC)),
                  _const_spec((1, LANES)),
                  _const_spec((1, LANES)),
                  _const_spec((SUBLANES, 1)),
                  _const_spec((SUBLANES, 1)),
                  _const_spec((1, GROUP_W)),
                  _const_spec((1, GROUP_W)),
                  _const_spec((SSM_CHUNK, SSM_CHUNK)),
                  _const_spec((SSM_CHUNK, SSM_CHUNK))],
        out_specs=pl.BlockSpec((MIX_T, GROUP_W), row),
        out_shape=jax.ShapeDtypeStruct((batch * seq, GROUP_W), BF16),
        scratch_shapes=[pltpu.VMEM((MIX_T + HALO, SSM_XBC), F32),
                        pltpu.VMEM((N_HEADS * SSM_STATE, GROUP_W), F32)],
        compiler_params=_params(2),
        name="ssd",
    )(ssd_in, small, small_t, conv_w, conv_b, dtb_col, a_col, dtb_row, a_row, d_skip, norm_g, ltri, utri)


def _sc_body(x_ref, cw_ref, ng_ref, e_ref, o_ref, u_scr):
    t = SC_T

    @pl.when(pl.program_id(1) == 0)
    def _():
        u_scr[0:HALO, :] = jnp.zeros((HALO, GROUP_W), F32)

    b_gate = x_ref[:, 0:GROUP_W].astype(F32)
    c_gate = x_ref[:, GROUP_W:2 * GROUP_W].astype(F32)
    val = x_ref[:, 2 * GROUP_W:3 * GROUP_W].astype(F32)
    u_scr[HALO:HALO + t, :] = c_gate * val
    conv = jnp.zeros((t, GROUP_W), F32)
    for kk in range(SC_CONV):
        conv = conv + cw_ref[kk:kk + 1, :] * u_scr[pl.ds(HALO - (SC_CONV - 1) + kk, t), :]
    u_scr[0:HALO, :] = u_scr[t:t + HALO, :]
    y = b_gate * conv
    ssq = _dot_x01(y * y, e_ref[...], 2)
    o_ref[...] = (y * lax.rsqrt(ssq * (1.0 / (GROUP_W // SC_GROUPS)) + EPS) * ng_ref[...]).astype(BF16)


def _short_conv(sc_in, conv_w, norm_g, e64, batch, seq):
    nt = seq // SC_T
    row = lambda b, t: (b * nt + t, 0)
    return pl.pallas_call(
        _sc_body,
        grid=(batch, nt),
        in_specs=[pl.BlockSpec((SC_T, 3 * GROUP_W), row),
                  _const_spec((SC_CONV, GROUP_W)),
                  _const_spec((1, GROUP_W)),
                  _const_spec((GROUP_W, GROUP_W))],
        out_specs=pl.BlockSpec((SC_T, GROUP_W), row),
        out_shape=jax.ShapeDtypeStruct((batch * seq, GROUP_W), BF16),
        scratch_shapes=[pltpu.VMEM((SC_T + HALO, GROUP_W), F32)],
        compiler_params=_params(2),
        name="short_conv",
    )(sc_in, conv_w, norm_g, e64)


def _outproj_body(x_ref, ya_ref, yb_ref, yc_ref, yd_ref, w_ref, o_ref):
    acc = x_ref[...]
    for i, y_ref in enumerate((ya_ref, yb_ref, yc_ref, yd_ref)):
        acc = acc + jnp.dot(y_ref[...], w_ref[i * GROUP_W:(i + 1) * GROUP_W, :],
                            preferred_element_type=F32)
    o_ref[...] = acc


def _outproj(x, ya, yb, yc, yd, w_out):
    m = x.shape[0]
    row = lambda i: (i, 0)
    y_spec = pl.BlockSpec((OUT_TM, GROUP_W), row)
    return pl.pallas_call(
        _outproj_body,
        grid=(m // OUT_TM,),
        in_specs=[pl.BlockSpec((OUT_TM, D_MODEL), row), y_spec, y_spec, y_spec, y_spec,
                  _const_spec((4 * GROUP_W, D_MODEL))],
        out_specs=pl.BlockSpec((OUT_TM, D_MODEL), row),
        out_shape=jax.ShapeDtypeStruct((m, D_MODEL), F32),
        compiler_params=_params(1),
        name="outproj",
    )(x, ya, yb, yc, yd, w_out)


def _block_diag_ones(size, block):
    idx = np.arange(size) // block
    return jnp.asarray(idx[:, None] == idx[None, :], BF16)


def _chunk_lower_tri(size, block):
    i = np.arange(size)
    same = (i[:, None] // block) == (i[None, :] // block)
    return jnp.asarray(same & (i[None, :] <= i[:, None]), BF16)


def _row(v):
    return v.reshape(1, -1).astype(F32)


def _lane_slot(values, offset, width):
    return jnp.zeros((1, width), F32).at[0, offset:offset + values.shape[0]].set(values.astype(F32))


def _sublane_col(values):
    return jnp.zeros((SUBLANES, 1), F32).at[0:values.shape[0], 0].set(values.astype(F32))


def _regroup_w_in(w_in):
    gla_end = 4 * GROUP_W
    fox0 = GLA_COLS
    fox_end = fox0 + 3 * GROUP_W
    ssm0 = GLA_COLS + FOX_COLS
    ssm_end = ssm0 + GROUP_W + SSM_XBC
    sc0 = GLA_COLS + FOX_COLS + SSM_COLS
    w_main = jnp.concatenate([w_in[:, 0:gla_end], w_in[:, fox0:fox_end],
                              w_in[:, ssm0:ssm_end], w_in[:, sc0:]], axis=1).astype(BF16)
    w_small = jnp.zeros((D_MODEL, LANES), F32)
    w_small = w_small.at[:, SMALL_LR:SMALL_LR + GLA_GATE_RANK].set(w_in[:, gla_end:GLA_COLS])
    w_small = w_small.at[:, SMALL_F:SMALL_F + N_HEADS].set(w_in[:, fox_end:ssm0])
    w_small = w_small.at[:, SMALL_DT:SMALL_DT + N_HEADS].set(w_in[:, ssm_end:sc0])
    return w_main, w_small.astype(BF16)


def kernel(x, ffn1_norm, ffn1_w_gate, ffn1_w_up, ffn1_w_down, mix_norm, w_in, gla_w_gate_up, gla_b_gate, gla_norm, fox_b_forget, fox_q_norm, fox_k_norm, fox_out_norm, ssm_conv_w, ssm_conv_b, ssm_dt_bias, ssm_A_log, ssm_D, ssm_norm, sc_conv_w, sc_out_norm, w_out, ffn2_norm, ffn2_w_gate, ffn2_w_up, ffn2_w_down):
    batch, seq, _ = x.shape
    depth = w_in.shape[0]
    assert seq % MIX_T == 0 and seq % SC_T == 0 and seq % ATT_TK == 0
    assert (batch * seq) % OUT_TM == 0

    e64 = _block_diag_ones(GROUP_W, HEAD_DIM)
    lb_gla = _chunk_lower_tri(MIX_T, GLA_CHUNK)
    utri_t = jnp.asarray(np.triu(np.ones((MIX_T, MIX_T))), BF16)
    ltri_q = jnp.asarray(np.tril(np.ones((SSM_CHUNK, SSM_CHUNK))), BF16)
    utri_q = jnp.asarray(np.triu(np.ones((SSM_CHUNK, SSM_CHUNK))), BF16)

    xf = x.reshape(batch * seq, D_MODEL)
    for l in range(depth):
        xf = _ffn(xf, _row(ffn1_norm[l]), ffn1_w_gate[l].astype(BF16),
                  ffn1_w_up[l].astype(BF16), ffn1_w_down[l].astype(BF16))

        w_main, w_small = _regroup_w_in(w_in[l])
        gla_in, fox_in, ssd_in, sc_in, small, small_t = _inproj(xf, _row(mix_norm[l]), w_main, w_small)

        wup_pad = jnp.zeros((LANES, GROUP_W), F32).at[SMALL_LR:SMALL_LR + GLA_GATE_RANK].set(
            gla_w_gate_up[l]).astype(BF16)
        ya = _gla(gla_in, small, wup_pad, _row(gla_b_gate[l]), _row(gla_norm[l]), lb_gla, e64, batch, seq)

        qn, kn, f_cum = _foxprep(fox_in, small_t, _sublane_col(fox_b_forget[l]),
                                 _row(jnp.tile(fox_q_norm[l], N_HEADS)),
                                 _row(jnp.tile(fox_k_norm[l], N_HEADS)), e64, utri_t, batch, seq)
        yb = _attn(qn, kn, fox_in, f_cum, _row(fox_out_norm[l]), batch, seq)

        a_neg = -jnp.exp(ssm_A_log[l].astype(F32))
        yc = _ssd(ssd_in, small, small_t, ssm_conv_w[l].astype(F32), _row(ssm_conv_b[l]),
                  _lane_slot(ssm_dt_bias[l], SMALL_DT, LANES), _lane_slot(a_neg, SMALL_DT, LANES),
                  _sublane_col(ssm_dt_bias[l]), _sublane_col(a_neg),
                  _row(jnp.repeat(ssm_D[l], HEAD_DIM)), _row(ssm_norm[l]), ltri_q, utri_q, batch, seq)

        yd = _short_conv(sc_in, sc_conv_w[l].astype(F32), _row(sc_out_norm[l]), e64, batch, seq)

        xf = _outproj(xf, ya, yb, yc, yd, w_out[l].astype(BF16))
        xf = _ffn(xf, _row(ffn2_norm[l]), ffn2_w_gate[l].astype(BF16),
                  ffn2_w_up[l].astype(BF16), ffn2_w_down[l].astype(BF16))
    return xf.reshape(batch, seq, D_MODEL)
```

```python
import functools

import numpy as np
import jax
import jax.numpy as jnp
from jax import lax
from jax.experimental import pallas as pl
from jax.experimental.pallas import tpu as pltpu

F32, BF16 = jnp.float32, jnp.bfloat16

D_MODEL = 1024
GROUP_W = 256
HEAD_DIM = 64
N_HEADS = GROUP_W // HEAD_DIM
GLA_GATE_RANK = 16
GLA_GATE_NORMALIZER = 16.0
GLA_CHUNK = 64
SSM_GROUPS = 2
SSM_STATE = 128
SSM_CONV = 4
SSM_CHUNK = 128
SSM_XBC = GROUP_W + 2 * SSM_GROUPS * SSM_STATE
SC_GROUPS = 4
SC_CONV = 3
D_FF = 2816
EPS = 1e-6
LOG2_E = 1.4426950408889634

GLA_COLS = 4 * GROUP_W + GLA_GATE_RANK
FOX_COLS = 3 * GROUP_W + N_HEADS
SSM_COLS = GROUP_W + SSM_XBC + N_HEADS

COL_GLA = 0
COL_FOX = COL_GLA + 4 * GROUP_W
COL_SSD = COL_FOX + 3 * GROUP_W
COL_SC = COL_SSD + GROUP_W + SSM_XBC
MAIN_COLS = COL_SC + 3 * GROUP_W

LANES = 128
SUBLANES = 8
SMALL_LR = 0
SMALL_F = 16
SMALL_DT = 24

FFN_TM = 512
FF_CHUNK = 256
PROJ_TM = 512
MIX_T = 512
ATT_TQ = 512
ATT_TK = 512
OUT_TM = 1024
HALO = 8

VMEM_LIMIT = 56 * 1024 * 1024


def _const_spec(shape):
    nd = len(shape)
    return pl.BlockSpec(shape, lambda *_: (0,) * nd, pipeline_mode=pl.Buffered(1))


def _params(n_axes):
    return pltpu.CompilerParams(dimension_semantics=("arbitrary",) * n_axes,
                                vmem_limit_bytes=VMEM_LIMIT)


def _split_bf16(x, pieces):
    out, r = [], x
    for i in range(pieces):
        p = r.astype(BF16)
        out.append(p)
        if i + 1 < pieces:
            r = r - p.astype(F32)
    return out


def _dot_x01(x, m01, pieces):
    acc = None
    for p in _split_bf16(x, pieces):
        d = jnp.dot(p, m01, preferred_element_type=F32)
        acc = d if acc is None else acc + d
    return acc


def _dot_01x(m01, x, pieces):
    acc = None
    for p in _split_bf16(x, pieces):
        d = jnp.dot(m01, p, preferred_element_type=F32)
        acc = d if acc is None else acc + d
    return acc


def _dot_nt(a, b):
    return lax.dot_general(a, b, (((1,), (1,)), ((), ())), preferred_element_type=F32)


def _dot_tn(a, b):
    return lax.dot_general(a, b, (((0,), (0,)), ((), ())), preferred_element_type=F32)


def _log_sigmoid(x):
    return jnp.minimum(x, 0.0) - jnp.log1p(jnp.exp(-jnp.abs(x)))


def _softplus(x):
    return jnp.maximum(x, 0.0) + jnp.log1p(jnp.exp(-jnp.abs(x)))


def _silu(x):
    return x * jax.nn.sigmoid(x)


def _rms_rows(x, gain):
    ms = jnp.mean(x * x, axis=-1, keepdims=True)
    return x * lax.rsqrt(ms + EPS) * gain


def _group_rms(x, ones_blocks, width, gain):
    ssq = _dot_x01(x * x, ones_blocks, 2)
    return x * lax.rsqrt(ssq * (1.0 / width) + EPS) * gain


def _head_id(shape, axis, width):
    return lax.broadcasted_iota(jnp.int32, shape, axis) // width


def _ffn_body(x_ref, g_ref, wg_ref, wu_ref, wd_ref, o_ref, a_scr):
    x = x_ref[...]
    h = _rms_rows(x, g_ref[...]).astype(BF16)
    for j in range(D_FF // FF_CHUNK):
        sl = slice(j * FF_CHUNK, (j + 1) * FF_CHUNK)
        gate = jnp.dot(h, wg_ref[:, sl], preferred_element_type=F32)
        up = jnp.dot(h, wu_ref[:, sl], preferred_element_type=F32)
        a_scr[:, sl] = (_silu(gate) * up).astype(BF16)
    y = jnp.dot(a_scr[...], wd_ref[...], preferred_element_type=F32)
    o_ref[...] = x + 0.5 * y


def _ffn(x, gain, wg, wu, wd):
    m = x.shape[0]
    row = lambda i: (i, 0)
    return pl.pallas_call(
        _ffn_body,
        grid=(m // FFN_TM,),
        in_specs=[pl.BlockSpec((FFN_TM, D_MODEL), row),
                  _const_spec((1, D_MODEL)),
                  _const_spec((D_MODEL, D_FF)),
                  _const_spec((D_MODEL, D_FF)),
                  _const_spec((D_FF, D_MODEL))],
        out_specs=pl.BlockSpec((FFN_TM, D_MODEL), row),
        out_shape=jax.ShapeDtypeStruct((m, D_MODEL), F32),
        scratch_shapes=[pltpu.VMEM((FFN_TM, D_FF), BF16)],
        compiler_params=_params(1),
        name="ffn",
    )(x, gain, wg, wu, wd)


def _inproj_body(x_ref, g_ref, wm_ref, ws_ref, bf_ref, qg_ref, kg_ref, e_ref, ut_ref,
                 cw_ref, cb_ref, scw_ref, scg_ref,
                 gla_ref, gvt_ref, qn_ref, kn_ref, vf_ref, f_ref, ssd_ref, sbt_ref, yd_ref, small_ref,
                 xb_scr, u_scr, f_carry, *, tiles_per_seq):
    t = PROJ_TM

    @pl.when(pl.program_id(0) % tiles_per_seq == 0)
    def _():
        xb_scr[0:HALO, :] = jnp.zeros((HALO, SSM_XBC), F32)
        u_scr[0:HALO, :] = jnp.zeros((HALO, GROUP_W), F32)
        f_carry[...] = jnp.zeros_like(f_carry)

    h = _rms_rows(x_ref[...], g_ref[...]).astype(BF16)

    def proj(col):
        return jnp.dot(h, wm_ref[:, col:col + GROUP_W], preferred_element_type=F32)

    for c in range(4):
        pc = proj(COL_GLA + c * GROUP_W)
        gla_ref[:, c * GROUP_W:(c + 1) * GROUP_W] = pc.astype(BF16)
        if c == 2:
            gvt_ref[...] = pc.T.astype(BF16)

    q_scale = HEAD_DIM ** -0.5 * LOG2_E
    qn_ref[...] = (_group_rms(proj(COL_FOX), e_ref[...], HEAD_DIM, qg_ref[...]) * q_scale).astype(BF16)
    kn_ref[...] = _group_rms(proj(COL_FOX + GROUP_W), e_ref[...], HEAD_DIM, kg_ref[...]).astype(BF16)
    vf_ref[...] = proj(COL_FOX + 2 * GROUP_W).astype(BF16)

    ssd_ref[:, 0:GROUP_W] = proj(COL_SSD).astype(BF16)
    for c in range(SSM_XBC // GROUP_W):
        cols = slice(c * GROUP_W, (c + 1) * GROUP_W)
        xb_scr[HALO:HALO + t, cols] = proj(COL_SSD + GROUP_W + c * GROUP_W)
    for c in range(SSM_XBC // GROUP_W):
        cols = slice(c * GROUP_W, (c + 1) * GROUP_W)
        conv = cb_ref[:, cols]
        for kk in range(SSM_CONV):
            conv = conv + cw_ref[kk:kk + 1, cols] * xb_scr[pl.ds(HALO - (SSM_CONV - 1) + kk, t), cols]
        act = _silu(conv)
        ssd_ref[:, GROUP_W + c * GROUP_W:2 * GROUP_W + c * GROUP_W] = act.astype(BF16)
        if c == 1:
            sbt_ref[...] = act.T.astype(BF16)
    xb_scr[0:HALO, :] = xb_scr[t:t + HALO, :]

    b_gate = proj(COL_SC)
    u_scr[HALO:HALO + t, :] = proj(COL_SC + GROUP_W) * proj(COL_SC + 2 * GROUP_W)
    conv = jnp.zeros((t, GROUP_W), F32)
    for kk in range(SC_CONV):
        conv = conv + scw_ref[kk:kk + 1, :] * u_scr[pl.ds(HALO - (SC_CONV - 1) + kk, t), :]
    u_scr[0:HALO, :] = u_scr[t:t + HALO, :]
    yd_ref[...] = _group_rms(b_gate * conv, e_ref[...], GROUP_W // SC_GROUPS, scg_ref[...]).astype(BF16)

    small = jnp.dot(h, ws_ref[...], preferred_element_type=F32)
    small_ref[...] = small
    log_f = _log_sigmoid(small.T[SMALL_F:SMALL_F + SUBLANES, :] + bf_ref[...])
    f_ref[...] = _dot_x01(log_f, ut_ref[...], 3) + f_carry[:, 0:1]
    f_carry[...] = f_carry[...] + jnp.sum(log_f, axis=-1, keepdims=True)


def _inproj(x, gain, w_main, w_small, b_forget, q_gain, k_gain, e64, utri, conv_w, conv_b,
            sc_conv_w, sc_gain, seq):
    m = x.shape[0]
    t = PROJ_TM
    row = lambda i: (i, 0)
    col = lambda i: (0, i)
    out_shape = [jax.ShapeDtypeStruct((m, 4 * GROUP_W), BF16),
                 jax.ShapeDtypeStruct((GROUP_W, m), BF16),
                 jax.ShapeDtypeStruct((m, GROUP_W), BF16),
                 jax.ShapeDtypeStruct((m, GROUP_W), BF16),
                 jax.ShapeDtypeStruct((m, GROUP_W), BF16),
                 jax.ShapeDtypeStruct((SUBLANES, m), F32),
                 jax.ShapeDtypeStruct((m, GROUP_W + SSM_XBC), BF16),
                 jax.ShapeDtypeStruct((GROUP_W, m), BF16),
                 jax.ShapeDtypeStruct((m, GROUP_W), BF16),
                 jax.ShapeDtypeStruct((m, LANES), F32)]
    out_specs = [pl.BlockSpec((t, 4 * GROUP_W), row),
                 pl.BlockSpec((GROUP_W, t), col),
                 pl.BlockSpec((t, GROUP_W), row),
                 pl.BlockSpec((t, GROUP_W), row),
                 pl.BlockSpec((t, GROUP_W), row),
                 pl.BlockSpec((SUBLANES, t), col),
                 pl.BlockSpec((t, GROUP_W + SSM_XBC), row),
                 pl.BlockSpec((GROUP_W, t), col),
                 pl.BlockSpec((t, GROUP_W), row),
                 pl.BlockSpec((t, LANES), row)]
    return pl.pallas_call(
        functools.partial(_inproj_body, tiles_per_seq=seq // t),
        grid=(m // t,),
        in_specs=[pl.BlockSpec((t, D_MODEL), row),
                  _const_spec((1, D_MODEL)),
                  _const_spec((D_MODEL, MAIN_COLS)),
                  _const_spec((D_MODEL, LANES)),
                  _const_spec((SUBLANES, 1)),
                  _const_spec((1, GROUP_W)),
                  _const_spec((1, GROUP_W)),
                  _const_spec((GROUP_W, GROUP_W)),
                  _const_spec((t, t)),
                  _const_spec((SSM_CONV, SSM_XBC)),
                  _const_spec((1, SSM_XBC)),
                  _const_spec((SC_CONV, GROUP_W)),
                  _const_spec((1, GROUP_W))],
        out_specs=out_specs,
        out_shape=out_shape,
        scratch_shapes=[pltpu.VMEM((t + HALO, SSM_XBC), F32),
                        pltpu.VMEM((t + HALO, GROUP_W), F32),
                        pltpu.VMEM((SUBLANES, LANES), F32)],
        compiler_params=_params(1),
        name="inproj",
    )(x, gain, w_main, w_small, b_forget, q_gain, k_gain, e64, utri, conv_w, conv_b, sc_conv_w, sc_gain)


def _gla_body(x_ref, vt_ref, sm_ref, wup_ref, bg_ref, ng_ref, lb_ref, e_ref, o_ref, st_scr, o_scr):
    t = MIX_T

    @pl.when(pl.program_id(1) == 0)
    def _():
        st_scr[...] = jnp.zeros_like(st_scr)

    q = x_ref[:, 0:GROUP_W].astype(F32) * (HEAD_DIM ** -0.5)
    k = x_ref[:, GROUP_W:2 * GROUP_W].astype(F32)
    v = x_ref[:, 2 * GROUP_W:3 * GROUP_W]
    g_out = x_ref[:, 3 * GROUP_W:4 * GROUP_W].astype(F32)

    z = jnp.dot(sm_ref[...].astype(BF16), wup_ref[...], preferred_element_type=F32) + bg_ref[...]
    log_a = _log_sigmoid(z) * (1.0 / GLA_GATE_NORMALIZER)
    b = _dot_01x(lb_ref[...], log_a, 3)

    c = GLA_CHUNK
    head_rows = _head_id((GROUP_W, GROUP_W), 0, HEAD_DIM)
    head_cols = _head_id((GROUP_W, GROUP_W), 1, HEAD_DIM)
    block_diag = head_rows == head_cols
    lane_head = _head_id((c, GROUP_W), 1, HEAD_DIM)
    row_id = lax.broadcasted_iota(jnp.int32, (c, GROUP_W), 0)
    key_id = lax.broadcasted_iota(jnp.int32, (c, GROUP_W), 1) % HEAD_DIM
    causal = key_id <= row_id

    for ci in range(t // c):
        r = slice(ci * c, (ci + 1) * c)
        bc = b[r]
        bl = bc[c - 1:c, :]
        qd = (q[r] * jnp.exp(bc)).astype(BF16)
        kd = (k[r] * jnp.exp(-bc)).astype(BF16)
        ke = (k[r] * jnp.exp(bl - bc)).astype(BF16)
        vb = v[r]
        zero = jnp.zeros_like(kd)
        k_stack = jnp.concatenate([jnp.where(lane_head == h, kd, zero) for h in range(N_HEADS)], axis=0)
        v_stack = jnp.concatenate([jnp.where(lane_head == h, vb, zero) for h in range(N_HEADS)], axis=0)
        att = _dot_nt(qd, k_stack)
        att = jnp.where(causal, att, 0.0).astype(BF16)
        st = st_scr[...]
        o_scr[r, :] = (jnp.dot(att, v_stack, preferred_element_type=F32)
                       + _dot_nt(qd, st.astype(BF16)))
        half = jnp.zeros_like(ke)
        ke_pair = jnp.concatenate([ke, half] if ci % 2 == 0 else [half, ke], axis=0)
        kv_t = jnp.dot(vt_ref[:, (ci // 2) * LANES:(ci // 2 + 1) * LANES], ke_pair,
                       preferred_element_type=F32)
        st_scr[...] = st * jnp.exp(bl) + jnp.where(block_diag, kv_t, 0.0)

    y = _group_rms(o_scr[...], e_ref[...], HEAD_DIM, ng_ref[...]) * _silu(g_out)
    o_ref[...] = y.astype(BF16)


def _gla(gla_in, gla_vt, small, wup_pad, b_gate, norm_g, lb, e64, batch, seq):
    nt = seq // MIX_T
    row = lambda b, t: (b * nt + t, 0)
    return pl.pallas_call(
        _gla_body,
        grid=(batch, nt),
        in_specs=[pl.BlockSpec((MIX_T, 4 * GROUP_W), row),
                  pl.BlockSpec((GROUP_W, MIX_T), lambda b, t: (0, b * nt + t)),
                  pl.BlockSpec((MIX_T, LANES), row),
                  _const_spec((LANES, GROUP_W)),
                  _const_spec((1, GROUP_W)),
                  _const_spec((1, GROUP_W)),
                  _const_spec((MIX_T, MIX_T)),
                  _const_spec((GROUP_W, GROUP_W))],
        out_specs=pl.BlockSpec((MIX_T, GROUP_W), row),
        out_shape=jax.ShapeDtypeStruct((batch * seq, GROUP_W), BF16),
        scratch_shapes=[pltpu.VMEM((GROUP_W, GROUP_W), F32),
                        pltpu.VMEM((MIX_T, GROUP_W), F32)],
        compiler_params=_params(2),
        name="gla",
    )(gla_in, gla_vt, small, wup_pad, b_gate, norm_g, lb, e64)


def _attn_body(q_ref, k_ref, v_ref, f_ref, og_ref, o_ref, qs_scr, m_scr, acc_scr):
    tq, tk = ATT_TQ, ATT_TK
    i = pl.program_id(1)
    q0 = pl.multiple_of(i * tq, tq)

    q = q_ref[...]
    lane_head = _head_id((tq, GROUP_W), 1, HEAD_DIM)
    for h in range(N_HEADS):
        qs_scr[h * tq:(h + 1) * tq, :] = jnp.where(lane_head == h, q, jnp.zeros_like(q))
    m_scr[...] = jnp.full_like(m_scr, -jnp.inf)
    acc_scr[...] = jnp.zeros_like(acc_scr)

    f_here = f_ref[:, pl.ds(q0, LANES)][:, 0:1]
    key_lane_head = _head_id((tk, GROUP_W), 1, HEAD_DIM)

    def step(j, masked):
        c0 = pl.multiple_of(j * tk, tk)
        k = k_ref[pl.ds(c0, tk), :]
        v = v_ref[pl.ds(c0, tk), :]
        bias = (f_here - f_ref[:, pl.ds(c0, tk)]) * LOG2_E
        if masked:
            keep = (lax.broadcasted_iota(jnp.int32, (tq, tk), 1) + c0
                    <= lax.broadcasted_iota(jnp.int32, (tq, tk), 0) + q0)
        for h in range(N_HEADS):
            hs = slice(h * tq, (h + 1) * tq)
            s = _dot_nt(qs_scr[hs, :], k) + bias[h:h + 1, :]
            if masked:
                s = jnp.where(keep, s, -jnp.inf)
            m_prev = m_scr[hs]
            m_new = jnp.maximum(m_prev, jnp.max(s, axis=-1, keepdims=True))
            alpha = jnp.exp2(m_prev - m_new)
            p = jnp.exp2(s - jnp.tile(m_new, (1, tk // LANES))).astype(BF16)
            m_scr[hs] = m_new
            v_h = jnp.where(key_lane_head == (h + 1) % N_HEADS, jnp.ones_like(v), v)
            acc_scr[hs, :] = (acc_scr[hs, :] * jnp.tile(alpha, (1, GROUP_W // LANES))
                              + jnp.dot(p, v_h, preferred_element_type=F32))

    n_full = (i * tq) // tk

    def loop_body(j, carry):
        step(j, False)
        return carry

    lax.fori_loop(0, n_full, loop_body, 0)
    for jm in range(max(1, tq // tk)):
        step(n_full + jm, True)

    y = jnp.zeros((tq, GROUP_W), F32)
    for h in range(N_HEADS):
        acc = acc_scr[h * tq:(h + 1) * tq, :]
        row_sum = jnp.max(jnp.where(lane_head == (h + 1) % N_HEADS, acc, -jnp.inf), axis=-1, keepdims=True)
        oh = acc / row_sum
        mine = lane_head == h
        ssq = jnp.sum(jnp.where(mine, oh * oh, 0.0), axis=-1, keepdims=True)
        y = jnp.where(mine, oh * lax.rsqrt(ssq * (1.0 / HEAD_DIM) + EPS), y)
    o_ref[...] = (y * og_ref[...]).astype(BF16)


def _attn(qn, kn, vf, f_cum, out_gain, batch, seq):
    nq = seq // ATT_TQ
    rows = N_HEADS * ATT_TQ
    return pl.pallas_call(
        _attn_body,
        grid=(batch, nq),
        in_specs=[pl.BlockSpec((ATT_TQ, GROUP_W), lambda b, i: (b * nq + i, 0)),
                  pl.BlockSpec((seq, GROUP_W), lambda b, i: (b, 0)),
                  pl.BlockSpec((seq, GROUP_W), lambda b, i: (b, 0)),
                  pl.BlockSpec((SUBLANES, seq), lambda b, i: (0, b)),
                  _const_spec((1, GROUP_W))],
        out_specs=pl.BlockSpec((ATT_TQ, GROUP_W), lambda b, i: (b * nq + i, 0)),
        out_shape=jax.ShapeDtypeStruct((batch * seq, GROUP_W), BF16),
        scratch_shapes=[pltpu.VMEM((rows, GROUP_W), BF16),
                        pltpu.VMEM((rows, LANES), F32),
                        pltpu.VMEM((rows, GROUP_W), F32)],
        compiler_params=_params(2),
        name="fox_attn",
    )(qn, kn, vf, f_cum, out_gain)


def _ssd_body(x_ref, bt_ref, sm_ref, dtb_ref, an_ref, sel4_ref, seldt_ref, dskip_ref, ng_ref, ltri_ref,
              o_ref, st_scr):
    t, qc, n = MIX_T, SSM_CHUNK, SSM_STATE

    @pl.when(pl.program_id(1) == 0)
    def _():
        st_scr[...] = jnp.zeros_like(st_scr)

    gate = _silu(x_ref[:, 0:GROUP_W].astype(F32))
    xs = x_ref[:, GROUP_W:2 * GROUP_W].astype(F32)
    dt_col = _softplus(sm_ref[...] + dtb_ref[...])
    a_slab = _dot_x01(dt_col * an_ref[...], sel4_ref[...], 3)
    xd = (xs * _dot_x01(dt_col, seldt_ref[...], 2)).astype(BF16)

    lane_head = _head_id((qc, GROUP_W), 1, HEAD_DIM)
    tril = (lax.broadcasted_iota(jnp.int32, (qc, qc), 1)
            <= lax.broadcasted_iota(jnp.int32, (qc, qc), 0))
    heads_per_group = N_HEADS // SSM_GROUPS

    for ci in range(t // qc):
        r = slice(ci * qc, (ci + 1) * qc)
        acs = _dot_01x(ltri_ref[...], a_slab[r], 3)
        bt_grp = [bt_ref[g * n:(g + 1) * n, r] for g in range(SSM_GROUPS)]
        c_grp = [x_ref[r, 2 * GROUP_W + (SSM_GROUPS + g) * n:2 * GROUP_W + (SSM_GROUPS + g + 1) * n]
                 for g in range(SSM_GROUPS)]
        cb = [jnp.dot(c_grp[g], bt_grp[g], preferred_element_type=F32) for g in range(SSM_GROUPS)]

        scores, c_scaled, b_decayed, e_last = [], [], [], []
        for h in range(N_HEADS):
            g = h // heads_per_group
            ac = acs[:, h * LANES:(h + 1) * LANES]
            ac_t = ac.T
            lmat = jnp.exp(jnp.where(tril, ac - ac_t, -jnp.inf))
            scores.append((cb[g] * lmat).astype(BF16))
            a_last = ac[qc - 1:qc, :]
            e_last.append(jnp.exp(a_last))
            b_decayed.append((bt_grp[g].astype(F32) * jnp.exp(a_last - ac_t)).astype(BF16))
            c_scaled.append((c_grp[g].astype(F32) * jnp.exp(ac)).astype(BF16))

        xd_c = xd[r]
        zero = jnp.zeros_like(xd_c)
        xd_heads = [jnp.where(lane_head == h, xd_c, zero) for h in range(N_HEADS)]
        st = st_scr[...]
        lhs = jnp.concatenate(scores + c_scaled, axis=1)
        rhs = jnp.concatenate(xd_heads + [st.astype(BF16)], axis=0)
        y = jnp.dot(lhs, rhs, preferred_element_type=F32)
        for h in range(N_HEADS):
            hs = slice(h * n, (h + 1) * n)
            st_scr[hs, :] = (st[hs] * jnp.tile(e_last[h], (1, GROUP_W // LANES))
                             + jnp.dot(b_decayed[h], xd_heads[h], preferred_element_type=F32))

        y = (y + xs[r] * dskip_ref[...]) * gate[r]
        for g in range(SSM_GROUPS):
            gs = slice(g * LANES, (g + 1) * LANES)
            yg = y[:, gs]
            ms = jnp.mean(yg * yg, axis=-1, keepdims=True)
            o_ref[r, gs] = (yg * lax.rsqrt(ms + EPS) * ng_ref[:, gs]).astype(BF16)


def _ssd(ssd_in, ssd_bt, small, dtb, a_neg, sel4, seldt, d_skip, norm_g, ltri, batch, seq):
    nt = seq // MIX_T
    row = lambda b, t: (b * nt + t, 0)
    return pl.pallas_call(
        _ssd_body,
        grid=(batch, nt),
        in_specs=[pl.BlockSpec((MIX_T, GROUP_W + SSM_XBC), row),
                  pl.BlockSpec((GROUP_W, MIX_T), lambda b, t: (0, b * nt + t)),
                  pl.BlockSpec((MIX_T, LANES), row),
                  _const_spec((1, LANES)),
                  _const_spec((1, LANES)),
                  _const_spec((LANES, N_HEADS * LANES)),
                  _const_spec((LANES, GROUP_W)),
                  _const_spec((1, GROUP_W)),
                  _const_spec((1, GROUP_W)),
                  _const_spec((SSM_CHUNK, SSM_CHUNK))],
        out_specs=pl.BlockSpec((MIX_T, GROUP_W), row),
        out_shape=jax.ShapeDtypeStruct((batch * seq, GROUP_W), BF16),
        scratch_shapes=[pltpu.VMEM((N_HEADS * SSM_STATE, GROUP_W), F32)],
        compiler_params=_params(2),
        name="ssd",
    )(ssd_in, ssd_bt, small, dtb, a_neg, sel4, seldt, d_skip, norm_g, ltri)


def _outproj_body(x_ref, ya_ref, yb_ref, yc_ref, yd_ref, w_ref, o_ref):
    acc = x_ref[...]
    for i, y_ref in enumerate((ya_ref, yb_ref, yc_ref, yd_ref)):
        acc = acc + jnp.dot(y_ref[...], w_ref[i * GROUP_W:(i + 1) * GROUP_W, :],
                            preferred_element_type=F32)
    o_ref[...] = acc


def _outproj(x, ya, yb, yc, yd, w_out):
    m = x.shape[0]
    row = lambda i: (i, 0)
    y_spec = pl.BlockSpec((OUT_TM, GROUP_W), row)
    return pl.pallas_call(
        _outproj_body,
        grid=(m // OUT_TM,),
        in_specs=[pl.BlockSpec((OUT_TM, D_MODEL), row), y_spec, y_spec, y_spec, y_spec,
                  _const_spec((4 * GROUP_W, D_MODEL))],
        out_specs=pl.BlockSpec((OUT_TM, D_MODEL), row),
        out_shape=jax.ShapeDtypeStruct((m, D_MODEL), F32),
        compiler_params=_params(1),
        name="outproj",
    )(x, ya, yb, yc, yd, w_out)


def _block_diag_ones(size, block):
    idx = np.arange(size) // block
    return jnp.asarray(idx[:, None] == idx[None, :], BF16)


def _chunk_lower_tri(size, block):
    i = np.arange(size)
    same = (i[:, None] // block) == (i[None, :] // block)
    return jnp.asarray(same & (i[None, :] <= i[:, None]), BF16)


def _lane_spread(src_lane0, width):
    m = np.zeros((LANES, N_HEADS * width))
    for h in range(N_HEADS):
        m[src_lane0 + h, h * width:(h + 1) * width] = 1.0
    return jnp.asarray(m, BF16)


def _row(v):
    return v.reshape(1, -1).astype(F32)


def _lane_slot(values, offset, width):
    return jnp.zeros((1, width), F32).at[0, offset:offset + values.shape[0]].set(values.astype(F32))


def _sublane_col(values):
    return jnp.zeros((SUBLANES, 1), F32).at[0:values.shape[0], 0].set(values.astype(F32))


def _regroup_w_in(w_in):
    gla_end = 4 * GROUP_W
    fox0 = GLA_COLS
    fox_end = fox0 + 3 * GROUP_W
    ssm0 = GLA_COLS + FOX_COLS
    ssm_end = ssm0 + GROUP_W + SSM_XBC
    sc0 = GLA_COLS + FOX_COLS + SSM_COLS
    w_main = jnp.concatenate([w_in[:, 0:gla_end], w_in[:, fox0:fox_end],
                              w_in[:, ssm0:ssm_end], w_in[:, sc0:]], axis=1).astype(BF16)
    w_small = jnp.zeros((D_MODEL, LANES), F32)
    w_small = w_small.at[:, SMALL_LR:SMALL_LR + GLA_GATE_RANK].set(w_in[:, gla_end:GLA_COLS])
    w_small = w_small.at[:, SMALL_F:SMALL_F + N_HEADS].set(w_in[:, fox_end:ssm0])
    w_small = w_small.at[:, SMALL_DT:SMALL_DT + N_HEADS].set(w_in[:, ssm_end:sc0])
    return w_main, w_small.astype(BF16)


def kernel(x, ffn1_norm, ffn1_w_gate, ffn1_w_up, ffn1_w_down, mix_norm, w_in, gla_w_gate_up, gla_b_gate, gla_norm, fox_b_forget, fox_q_norm, fox_k_norm, fox_out_norm, ssm_conv_w, ssm_conv_b, ssm_dt_bias, ssm_A_log, ssm_D, ssm_norm, sc_conv_w, sc_out_norm, w_out, ffn2_norm, ffn2_w_gate, ffn2_w_up, ffn2_w_down):
    batch, seq, _ = x.shape
    depth = w_in.shape[0]
    assert seq % MIX_T == 0 and seq % PROJ_TM == 0 and seq % ATT_TK == 0 and seq % ATT_TQ == 0
    assert (batch * seq) % OUT_TM == 0 and (batch * seq) % FFN_TM == 0

    e64 = _block_diag_ones(GROUP_W, HEAD_DIM)
    lb_gla = _chunk_lower_tri(MIX_T, GLA_CHUNK)
    utri_t = jnp.asarray(np.triu(np.ones((PROJ_TM, PROJ_TM))), BF16)
    ltri_q = jnp.asarray(np.tril(np.ones((SSM_CHUNK, SSM_CHUNK))), BF16)
    sel4 = _lane_spread(SMALL_DT, LANES)
    seldt = _lane_spread(SMALL_DT, HEAD_DIM)

    xf = x.reshape(batch * seq, D_MODEL)
    for l in range(depth):
        xf = _ffn(xf, _row(ffn1_norm[l]), ffn1_w_gate[l].astype(BF16),
                  ffn1_w_up[l].astype(BF16), ffn1_w_down[l].astype(BF16))

        w_main, w_small = _regroup_w_in(w_in[l])
        gla_in, gla_vt, qn, kn, vf, f_cum, ssd_in, ssd_bt, yd, small = _inproj(
            xf, _row(mix_norm[l]), w_main, w_small, _sublane_col(fox_b_forget[l]),
            _row(jnp.tile(fox_q_norm[l], N_HEADS)), _row(jnp.tile(fox_k_norm[l], N_HEADS)),
            e64, utri_t, ssm_conv_w[l].astype(F32), _row(ssm_conv_b[l]),
            sc_conv_w[l].astype(F32), _row(sc_out_norm[l]), seq)

        wup_pad = jnp.zeros((LANES, GROUP_W), F32).at[SMALL_LR:SMALL_LR + GLA_GATE_RANK].set(
            gla_w_gate_up[l]).astype(BF16)
        ya = _gla(gla_in, gla_vt, small, wup_pad, _row(gla_b_gate[l]), _row(gla_norm[l]), lb_gla, e64, batch, seq)

        yb = _attn(qn, kn, vf, f_cum, _row(fox_out_norm[l]), batch, seq)

        a_neg = -jnp.exp(ssm_A_log[l].astype(F32))
        yc = _ssd(ssd_in, ssd_bt, small, _lane_slot(ssm_dt_bias[l], SMALL_DT, LANES),
                  _lane_slot(a_neg, SMALL_DT, LANES), sel4, seldt,
                  _row(jnp.repeat(ssm_D[l], HEAD_DIM)), _row(ssm_norm[l]), ltri_q, batch, seq)

        xf = _outproj(xf, ya, yb, yc, yd, w_out[l].astype(BF16))
        xf = _ffn(xf, _row(ffn2_norm[l]), ffn2_w_gate[l].astype(BF16),
                  ffn2_w_up[l].astype(BF16), ffn2_w_down[l].astype(BF16))
    return xf.reshape(batch, seq, D_MODEL)
```

```python
import functools

import numpy as np
import jax
import jax.numpy as jnp
from jax import lax
from jax.experimental import pallas as pl
from jax.experimental.pallas import tpu as pltpu

F32, BF16 = jnp.float32, jnp.bfloat16

D_MODEL = 1024
GROUP_W = 256
HEAD_DIM = 64
N_HEADS = GROUP_W // HEAD_DIM
GLA_GATE_RANK = 16
GLA_GATE_NORMALIZER = 16.0
GLA_CHUNK = 64
SSM_GROUPS = 2
SSM_STATE = 128
SSM_CONV = 4
SSM_CHUNK = 128
SSM_XBC = GROUP_W + 2 * SSM_GROUPS * SSM_STATE
SC_GROUPS = 4
SC_CONV = 3
D_FF = 2816
EPS = 1e-6
LOG2_E = 1.4426950408889634

GLA_COLS = 4 * GROUP_W + GLA_GATE_RANK
FOX_COLS = 3 * GROUP_W + N_HEADS
SSM_COLS = GROUP_W + SSM_XBC + N_HEADS

COL_GLA = 0
COL_FOX = COL_GLA + 4 * GROUP_W
COL_SSD = COL_FOX + 3 * GROUP_W
COL_SC = COL_SSD + GROUP_W + SSM_XBC
MAIN_COLS = COL_SC + 3 * GROUP_W

LANES = 128
SUBLANES = 8
SMALL_LR = 0
SMALL_F = 16
SMALL_DT = 24

FFN_TM = 512
FF_CHUNK = 256
PROJ_TM = 512
MIX_T = 512
ATT_TQ = 512
ATT_TK = 512
CAST_ROW_BLOCKS = 4
HALO = 8

VMEM_LIMIT = 56 * 1024 * 1024


def _const_spec(shape):
    nd = len(shape)
    return pl.BlockSpec(shape, lambda *_: (0,) * nd, pipeline_mode=pl.Buffered(1))


def _params(n_axes):
    return pltpu.CompilerParams(dimension_semantics=("arbitrary",) * n_axes,
                                vmem_limit_bytes=VMEM_LIMIT)


def _split_bf16(x, pieces):
    out, r = [], x
    for i in range(pieces):
        p = r.astype(BF16)
        out.append(p)
        if i + 1 < pieces:
            r = r - p.astype(F32)
    return out


def _dot_x01(x, m01, pieces):
    acc = None
    for p in _split_bf16(x, pieces):
        d = jnp.dot(p, m01, preferred_element_type=F32)
        acc = d if acc is None else acc + d
    return acc


def _dot_01x(m01, x, pieces):
    acc = None
    for p in _split_bf16(x, pieces):
        d = jnp.dot(m01, p, preferred_element_type=F32)
        acc = d if acc is None else acc + d
    return acc


def _dot_nt(a, b):
    return lax.dot_general(a, b, (((1,), (1,)), ((), ())), preferred_element_type=F32)


def _log_sigmoid(x):
    return jnp.minimum(x, 0.0) - jnp.log1p(jnp.exp(-jnp.abs(x)))


def _softplus(x):
    return jnp.maximum(x, 0.0) + jnp.log1p(jnp.exp(-jnp.abs(x)))


def _silu(x):
    return x * jax.nn.sigmoid(x)


def _rms_rows(x, gain):
    ms = jnp.mean(x * x, axis=-1, keepdims=True)
    return x * lax.rsqrt(ms + EPS) * gain


def _group_rms(x, ones_blocks, width, gain):
    ssq = _dot_x01(x * x, ones_blocks, 2)
    return x * lax.rsqrt(ssq * (1.0 / width) + EPS) * gain


def _head_id(shape, axis, width):
    return lax.broadcasted_iota(jnp.int32, shape, axis) // width


def _swiglu_half_step(x, g_ref, wg_ref, wu_ref, wd_ref, a_scr):
    h = _rms_rows(x, g_ref[...]).astype(BF16)
    for j in range(D_FF // FF_CHUNK):
        sl = slice(j * FF_CHUNK, (j + 1) * FF_CHUNK)
        gate = jnp.dot(h, wg_ref[:, sl], preferred_element_type=F32)
        up = jnp.dot(h, wu_ref[:, sl], preferred_element_type=F32)
        a_scr[:, sl] = (_silu(gate) * up).astype(BF16)
    return x + 0.5 * jnp.dot(a_scr[...], wd_ref[...], preferred_element_type=F32)


def _ffn_body(x_ref, g_ref, wg_ref, wu_ref, wd_ref, o_ref, a_scr):
    o_ref[...] = _swiglu_half_step(x_ref[...], g_ref, wg_ref, wu_ref, wd_ref, a_scr)


def _mix_ffn_body(x_ref, ya_ref, yb_ref, yc_ref, yd_ref, wo_ref, g_ref, wg_ref, wu_ref, wd_ref, o_ref, a_scr):
    x = x_ref[...]
    for i, y_ref in enumerate((ya_ref, yb_ref, yc_ref, yd_ref)):
        x = x + jnp.dot(y_ref[...], wo_ref[i * GROUP_W:(i + 1) * GROUP_W, :], preferred_element_type=F32)
    o_ref[...] = _swiglu_half_step(x, g_ref, wg_ref, wu_ref, wd_ref, a_scr)


def _ffn(x, gain, wg, wu, wd, mixers=None, w_out=None):
    m = x.shape[0]
    row = lambda i: (i, 0)
    ffn_specs = [_const_spec((1, D_MODEL)),
                 _const_spec((D_MODEL, D_FF)),
                 _const_spec((D_MODEL, D_FF)),
                 _const_spec((D_FF, D_MODEL))]
    if mixers is None:
        body, name, lead, lead_specs = _ffn_body, "ffn", (), []
    else:
        body, name, lead = _mix_ffn_body, "mix_ffn", tuple(mixers) + (w_out,)
        lead_specs = [pl.BlockSpec((FFN_TM, GROUP_W), row)] * len(mixers) + [_const_spec((4 * GROUP_W, D_MODEL))]
    return pl.pallas_call(
        body,
        grid=(m // FFN_TM,),
        in_specs=[pl.BlockSpec((FFN_TM, D_MODEL), row)] + lead_specs + ffn_specs,
        out_specs=pl.BlockSpec((FFN_TM, D_MODEL), row),
        out_shape=jax.ShapeDtypeStruct((m, D_MODEL), F32),
        scratch_shapes=[pltpu.VMEM((FFN_TM, D_FF), BF16)],
        compiler_params=_params(1),
        name=name,
    )(x, *lead, gain, wg, wu, wd)


def _cast_body(w_ref, o_ref):
    o_ref[...] = w_ref[...].astype(BF16)


def _to_bf16(w):
    depth, rows, cols = w.shape
    block_rows = rows // CAST_ROW_BLOCKS
    assert block_rows * CAST_ROW_BLOCKS == rows and block_rows % 16 == 0
    spec = pl.BlockSpec((1, block_rows, cols), lambda l, i: (l, i, 0))
    return pl.pallas_call(
        _cast_body,
        grid=(depth, CAST_ROW_BLOCKS),
        in_specs=[spec],
        out_specs=spec,
        out_shape=jax.ShapeDtypeStruct(w.shape, BF16),
        compiler_params=_params(2),
        name="cast_bf16",
    )(w)


def _inproj_body(x_ref, g_ref, wm_ref, ws_ref, bf_ref, qg_ref, kg_ref, e_ref, ut_ref,
                 cw_ref, cb_ref, scw_ref, scg_ref,
                 gla_ref, gvt_ref, qn_ref, kn_ref, vf_ref, f_ref, ssd_ref, sbt_ref, yd_ref, small_ref, dtr_ref,
                 xb_scr, u_scr, f_carry, *, tiles_per_seq):
    t = PROJ_TM

    @pl.when(pl.program_id(0) % tiles_per_seq == 0)
    def _():
        xb_scr[0:HALO, :] = jnp.zeros((HALO, SSM_XBC), F32)
        u_scr[0:HALO, :] = jnp.zeros((HALO, GROUP_W), F32)
        f_carry[...] = jnp.zeros_like(f_carry)

    h = _rms_rows(x_ref[...], g_ref[...]).astype(BF16)

    def proj(col):
        return jnp.dot(h, wm_ref[:, col:col + GROUP_W], preferred_element_type=F32)

    for c in range(4):
        pc = proj(COL_GLA + c * GROUP_W)
        gla_ref[:, c * GROUP_W:(c + 1) * GROUP_W] = pc.astype(BF16)
        if c == 2:
            gvt_ref[...] = pc.T.astype(BF16)

    q_scale = HEAD_DIM ** -0.5 * LOG2_E
    qn_ref[...] = (_group_rms(proj(COL_FOX), e_ref[...], HEAD_DIM, qg_ref[...]) * q_scale).astype(BF16)
    kn_ref[...] = _group_rms(proj(COL_FOX + GROUP_W), e_ref[...], HEAD_DIM, kg_ref[...]).astype(BF16)
    vf_ref[...] = proj(COL_FOX + 2 * GROUP_W).astype(BF16)

    ssd_ref[:, 0:GROUP_W] = proj(COL_SSD).astype(BF16)
    for c in range(SSM_XBC // GROUP_W):
        cols = slice(c * GROUP_W, (c + 1) * GROUP_W)
        xb_scr[HALO:HALO + t, cols] = proj(COL_SSD + GROUP_W + c * GROUP_W)
    for c in range(SSM_XBC // GROUP_W):
        cols = slice(c * GROUP_W, (c + 1) * GROUP_W)
        conv = cb_ref[:, cols]
        for kk in range(SSM_CONV):
            conv = conv + cw_ref[kk:kk + 1, cols] * xb_scr[pl.ds(HALO - (SSM_CONV - 1) + kk, t), cols]
        act = _silu(conv)
        ssd_ref[:, GROUP_W + c * GROUP_W:2 * GROUP_W + c * GROUP_W] = act.astype(BF16)
        if c == 1:
            sbt_ref[...] = act.T.astype(BF16)
    xb_scr[0:HALO, :] = xb_scr[t:t + HALO, :]

    b_gate = proj(COL_SC)
    u_scr[HALO:HALO + t, :] = proj(COL_SC + GROUP_W) * proj(COL_SC + 2 * GROUP_W)
    conv = jnp.zeros((t, GROUP_W), F32)
    for kk in range(SC_CONV):
        conv = conv + scw_ref[kk:kk + 1, :] * u_scr[pl.ds(HALO - (SC_CONV - 1) + kk, t), :]
    u_scr[0:HALO, :] = u_scr[t:t + HALO, :]
    yd_ref[...] = _group_rms(b_gate * conv, e_ref[...], GROUP_W // SC_GROUPS, scg_ref[...]).astype(BF16)

    small = jnp.dot(h, ws_ref[...], preferred_element_type=F32)
    small_ref[...] = small
    small_t = small.T
    dtr_ref[...] = small_t[SMALL_DT:SMALL_DT + SUBLANES, :]
    log_f = _log_sigmoid(small_t[SMALL_F:SMALL_F + SUBLANES, :] + bf_ref[...])
    f_ref[...] = _dot_x01(log_f, ut_ref[...], 3) + f_carry[:, 0:1]
    f_carry[...] = f_carry[...] + jnp.sum(log_f, axis=-1, keepdims=True)


def _inproj(x, gain, w_main, w_small, b_forget, q_gain, k_gain, e64, utri, conv_w, conv_b,
            sc_conv_w, sc_gain, seq):
    m = x.shape[0]
    t = PROJ_TM
    row = lambda i: (i, 0)
    col = lambda i: (0, i)
    out_shape = [jax.ShapeDtypeStruct((m, 4 * GROUP_W), BF16),
                 jax.ShapeDtypeStruct((GROUP_W, m), BF16),
                 jax.ShapeDtypeStruct((m, GROUP_W), BF16),
                 jax.ShapeDtypeStruct((m, GROUP_W), BF16),
                 jax.ShapeDtypeStruct((m, GROUP_W), BF16),
                 jax.ShapeDtypeStruct((SUBLANES, m), F32),
                 jax.ShapeDtypeStruct((m, GROUP_W + SSM_XBC), BF16),
                 jax.ShapeDtypeStruct((GROUP_W, m), BF16),
                 jax.ShapeDtypeStruct((m, GROUP_W), BF16),
                 jax.ShapeDtypeStruct((m, LANES), F32),
                 jax.ShapeDtypeStruct((SUBLANES, m), F32)]
    out_specs = [pl.BlockSpec((t, 4 * GROUP_W), row),
                 pl.BlockSpec((GROUP_W, t), col),
                 pl.BlockSpec((t, GROUP_W), row),
                 pl.BlockSpec((t, GROUP_W), row),
                 pl.BlockSpec((t, GROUP_W), row),
                 pl.BlockSpec((SUBLANES, t), col),
                 pl.BlockSpec((t, GROUP_W + SSM_XBC), row),
                 pl.BlockSpec((GROUP_W, t), col),
                 pl.BlockSpec((t, GROUP_W), row),
                 pl.BlockSpec((t, LANES), row),
                 pl.BlockSpec((SUBLANES, t), col)]
    return pl.pallas_call(
        functools.partial(_inproj_body, tiles_per_seq=seq // t),
        grid=(m // t,),
        in_specs=[pl.BlockSpec((t, D_MODEL), row),
                  _const_spec((1, D_MODEL)),
                  _const_spec((D_MODEL, MAIN_COLS)),
                  _const_spec((D_MODEL, LANES)),
                  _const_spec((SUBLANES, 1)),
                  _const_spec((1, GROUP_W)),
                  _const_spec((1, GROUP_W)),
                  _const_spec((GROUP_W, GROUP_W)),
                  _const_spec((t, t)),
                  _const_spec((SSM_CONV, SSM_XBC)),
                  _const_spec((1, SSM_XBC)),
                  _const_spec((SC_CONV, GROUP_W)),
                  _const_spec((1, GROUP_W))],
        out_specs=out_specs,
        out_shape=out_shape,
        scratch_shapes=[pltpu.VMEM((t + HALO, SSM_XBC), F32),
                        pltpu.VMEM((t + HALO, GROUP_W), F32),
                        pltpu.VMEM((SUBLANES, LANES), F32)],
        compiler_params=_params(1),
        name="inproj",
    )(x, gain, w_main, w_small, b_forget, q_gain, k_gain, e64, utri, conv_w, conv_b, sc_conv_w, sc_gain)


def _gla_body(x_ref, vt_ref, sm_ref, wup_ref, bg_ref, ng_ref, lb_ref, e_ref, o_ref, st_scr, o_scr):
    t = MIX_T

    @pl.when(pl.program_id(1) == 0)
    def _():
        st_scr[...] = jnp.zeros_like(st_scr)

    q = x_ref[:, 0:GROUP_W].astype(F32) * (HEAD_DIM ** -0.5)
    k = x_ref[:, GROUP_W:2 * GROUP_W].astype(F32)
    v = x_ref[:, 2 * GROUP_W:3 * GROUP_W]
    g_out = x_ref[:, 3 * GROUP_W:4 * GROUP_W].astype(F32)

    z = jnp.dot(sm_ref[...].astype(BF16), wup_ref[...], preferred_element_type=F32) + bg_ref[...]
    log_a = _log_sigmoid(z) * (1.0 / GLA_GATE_NORMALIZER)
    b = _dot_01x(lb_ref[...], log_a, 3)

    c = GLA_CHUNK
    head_rows = _head_id((GROUP_W, GROUP_W), 0, HEAD_DIM)
    head_cols = _head_id((GROUP_W, GROUP_W), 1, HEAD_DIM)
    block_diag = head_rows == head_cols
    lane_head = _head_id((c, GROUP_W), 1, HEAD_DIM)
    row_id = lax.broadcasted_iota(jnp.int32, (c, GROUP_W), 0)
    key_id = lax.broadcasted_iota(jnp.int32, (c, GROUP_W), 1) % HEAD_DIM
    causal = key_id <= row_id

    st = st_scr[...]
    for ci in range(t // c):
        r = slice(ci * c, (ci + 1) * c)
        bc = b[r]
        bl = bc[c - 1:c, :]
        qd = (q[r] * jnp.exp(bc)).astype(BF16)
        kd = (k[r] * jnp.exp(-bc)).astype(BF16)
        ke = (k[r] * jnp.exp(bl - bc)).astype(BF16)
        vb = v[r]
        zero = jnp.zeros_like(kd)
        k_stack = jnp.concatenate([jnp.where(lane_head == h, kd, zero) for h in range(N_HEADS)], axis=0)
        v_stack = jnp.concatenate([jnp.where(lane_head == h, vb, zero) for h in range(N_HEADS)], axis=0)
        att = _dot_nt(qd, k_stack)
        att = jnp.where(causal, att, 0.0).astype(BF16)
        o_scr[r, :] = (jnp.dot(att, v_stack, preferred_element_type=F32)
                       + _dot_nt(qd, st.astype(BF16)))
        half = jnp.zeros_like(ke)
        ke_pair = jnp.concatenate([ke, half] if ci % 2 == 0 else [half, ke], axis=0)
        kv_t = jnp.dot(vt_ref[:, (ci // 2) * LANES:(ci // 2 + 1) * LANES], ke_pair,
                       preferred_element_type=F32)
        st = st * jnp.exp(bl) + jnp.where(block_diag, kv_t, 0.0)
    st_scr[...] = st

    y = _group_rms(o_scr[...], e_ref[...], HEAD_DIM, ng_ref[...]) * _silu(g_out)
    o_ref[...] = y.astype(BF16)


def _gla(gla_in, gla_vt, small, wup_pad, b_gate, norm_g, lb, e64, batch, seq):
    nt = seq // MIX_T
    row = lambda b, t: (b * nt + t, 0)
    return pl.pallas_call(
        _gla_body,
        grid=(batch, nt),
        in_specs=[pl.BlockSpec((MIX_T, 4 * GROUP_W), row),
                  pl.BlockSpec((GROUP_W, MIX_T), lambda b, t: (0, b * nt + t)),
                  pl.BlockSpec((MIX_T, LANES), row),
                  _const_spec((LANES, GROUP_W)),
                  _const_spec((1, GROUP_W)),
                  _const_spec((1, GROUP_W)),
                  _const_spec((MIX_T, MIX_T)),
                  _const_spec((GROUP_W, GROUP_W))],
        out_specs=pl.BlockSpec((MIX_T, GROUP_W), row),
        out_shape=jax.ShapeDtypeStruct((batch * seq, GROUP_W), BF16),
        scratch_shapes=[pltpu.VMEM((GROUP_W, GROUP_W), F32),
                        pltpu.VMEM((MIX_T, GROUP_W), F32)],
        compiler_params=_params(2),
        name="gla",
    )(gla_in, gla_vt, small, wup_pad, b_gate, norm_g, lb, e64)


def _attn_body(q_ref, k_ref, v_ref, f_ref, og_ref, o_ref, qs_scr, m_scr, acc_scr, p_scr, al_scr):
    tq, tk = ATT_TQ, ATT_TK
    i = pl.program_id(1)
    q0 = pl.multiple_of(i * tq, tq)

    q = q_ref[...]
    lane_head = _head_id((tq, GROUP_W), 1, HEAD_DIM)
    for h in range(N_HEADS):
        qs_scr[h * tq:(h + 1) * tq, :] = jnp.where(lane_head == h, q, jnp.zeros_like(q))
    m_scr[...] = jnp.full_like(m_scr, -jnp.inf)
    acc_scr[...] = jnp.zeros_like(acc_scr)

    f_here = f_ref[:, pl.ds(q0, LANES)][:, 0:1]
    key_lane_head = _head_id((tk, GROUP_W), 1, HEAD_DIM)

    def scores_to_probs(j, masked):
        c0 = pl.multiple_of(j * tk, tk)
        k = k_ref[pl.ds(c0, tk), :]
        bias = (f_here - f_ref[:, pl.ds(c0, tk)]) * LOG2_E
        if masked:
            keep = (lax.broadcasted_iota(jnp.int32, (tq, tk), 1) + c0
                    <= lax.broadcasted_iota(jnp.int32, (tq, tk), 0) + q0)
        for h in range(N_HEADS):
            hs = slice(h * tq, (h + 1) * tq)
            s = _dot_nt(qs_scr[hs, :], k) + bias[h:h + 1, :]
            if masked:
                s = jnp.where(keep, s, -jnp.inf)
            m_prev = m_scr[hs]
            m_new = jnp.maximum(m_prev, jnp.max(s, axis=-1, keepdims=True))
            al_scr[hs] = jnp.exp2(m_prev - m_new)
            p_scr[hs, :] = jnp.exp2(s - jnp.tile(m_new, (1, tk // LANES))).astype(BF16)
            m_scr[hs] = m_new

    def accumulate(j):
        c0 = pl.multiple_of(j * tk, tk)
        v = v_ref[pl.ds(c0, tk), :]
        for h in range(N_HEADS):
            hs = slice(h * tq, (h + 1) * tq)
            v_h = jnp.where(key_lane_head == (h + 1) % N_HEADS, jnp.ones_like(v), v)
            acc_scr[hs, :] = (acc_scr[hs, :] * jnp.tile(al_scr[hs], (1, GROUP_W // LANES))
                              + jnp.dot(p_scr[hs, :], v_h, preferred_element_type=F32))

    n_full = (i * tq) // tk

    scores_to_probs(0, True)

    def loop_body(j, carry):
        accumulate(j - 1)
        scores_to_probs(j, False)
        return carry

    lax.fori_loop(1, n_full, loop_body, 0)

    @pl.when(n_full >= 1)
    def _():
        accumulate(n_full - 1)
        scores_to_probs(n_full, True)

    accumulate(n_full)

    y = jnp.zeros((tq, GROUP_W), F32)
    for h in range(N_HEADS):
        acc = acc_scr[h * tq:(h + 1) * tq, :]
        row_sum = jnp.max(jnp.where(lane_head == (h + 1) % N_HEADS, acc, -jnp.inf), axis=-1, keepdims=True)
        oh = acc / row_sum
        mine = lane_head == h
        ssq = jnp.sum(jnp.where(mine, oh * oh, 0.0), axis=-1, keepdims=True)
        y = jnp.where(mine, oh * lax.rsqrt(ssq * (1.0 / HEAD_DIM) + EPS), y)
    o_ref[...] = (y * og_ref[...]).astype(BF16)


def _attn(qn, kn, vf, f_cum, out_gain, batch, seq):
    assert ATT_TQ == ATT_TK, "the key-tile pipeline assumes exactly one tile on the causal diagonal"
    nq = seq // ATT_TQ
    rows = N_HEADS * ATT_TQ
    return pl.pallas_call(
        _attn_body,
        grid=(batch, nq),
        in_specs=[pl.BlockSpec((ATT_TQ, GROUP_W), lambda b, i: (b * nq + i, 0)),
                  pl.BlockSpec((seq, GROUP_W), lambda b, i: (b, 0)),
                  pl.BlockSpec((seq, GROUP_W), lambda b, i: (b, 0)),
                  pl.BlockSpec((SUBLANES, seq), lambda b, i: (0, b)),
                  _const_spec((1, GROUP_W))],
        out_specs=pl.BlockSpec((ATT_TQ, GROUP_W), lambda b, i: (b * nq + i, 0)),
        out_shape=jax.ShapeDtypeStruct((batch * seq, GROUP_W), BF16),
        scratch_shapes=[pltpu.VMEM((rows, GROUP_W), BF16),
                        pltpu.VMEM((rows, LANES), F32),
                        pltpu.VMEM((rows, GROUP_W), F32),
                        pltpu.VMEM((rows, ATT_TK), BF16),
                        pltpu.VMEM((rows, LANES), F32)],
        compiler_params=_params(2),
        name="fox_attn",
    )(qn, kn, vf, f_cum, out_gain)


def _ssd_body(x_ref, bt_ref, sm_ref, dtr_ref, dtb_ref, dtbr_ref, anr_ref, seldt_ref, dskip_ref, ng_ref, ub_ref,
              o_ref, st_scr):
    t, qc, n = MIX_T, SSM_CHUNK, SSM_STATE

    @pl.when(pl.program_id(1) == 0)
    def _():
        st_scr[...] = jnp.zeros_like(st_scr)

    gate = _silu(x_ref[:, 0:GROUP_W].astype(F32))
    xs = x_ref[:, GROUP_W:2 * GROUP_W].astype(F32)
    dt_col = _softplus(sm_ref[...] + dtb_ref[...])
    xd = (xs * _dot_x01(dt_col, seldt_ref[...], 2)).astype(BF16)

    a_row = _softplus(dtr_ref[...] + dtbr_ref[...]) * anr_ref[...]
    pieces = jnp.dot(jnp.concatenate(_split_bf16(a_row, 3), axis=0), ub_ref[...], preferred_element_type=F32)
    acs_row = pieces[0:SUBLANES] + pieces[SUBLANES:2 * SUBLANES] + pieces[2 * SUBLANES:3 * SUBLANES]

    lane_head = _head_id((qc, GROUP_W), 1, HEAD_DIM)
    tril = (lax.broadcasted_iota(jnp.int32, (qc, qc), 1)
            <= lax.broadcasted_iota(jnp.int32, (qc, qc), 0))
    heads_per_group = N_HEADS // SSM_GROUPS
    state_block = (_head_id((N_HEADS * n, GROUP_W), 0, n) == _head_id((N_HEADS * n, GROUP_W), 1, HEAD_DIM))

    for ci in range(t // qc):
        r = slice(ci * qc, (ci + 1) * qc)
        bt_grp = [bt_ref[g * n:(g + 1) * n, r] for g in range(SSM_GROUPS)]
        c_grp = [x_ref[r, 2 * GROUP_W + (SSM_GROUPS + g) * n:2 * GROUP_W + (SSM_GROUPS + g + 1) * n]
                 for g in range(SSM_GROUPS)]
        cb = [jnp.dot(c_grp[g], bt_grp[g], preferred_element_type=F32) for g in range(SSM_GROUPS)]

        scores, c_scaled, b_decayed, e_last = [], [], [], []
        for h in range(N_HEADS):
            g = h // heads_per_group
            ac_t = jnp.broadcast_to(acs_row[h:h + 1, r], (qc, qc))
            ac = ac_t.T
            lmat = jnp.exp(jnp.where(tril, ac - ac_t, -jnp.inf))
            scores.append((cb[g] * lmat).astype(BF16))
            a_last = ac[qc - 1:qc, :]
            e_last.append(jnp.exp(a_last))
            b_decayed.append((bt_grp[g].astype(F32) * jnp.exp(a_last - ac_t[0:1, :])).astype(BF16))
            c_scaled.append((c_grp[g].astype(F32) * jnp.exp(ac)).astype(BF16))

        xd_c = xd[r]
        zero = jnp.zeros_like(xd_c)
        xd_heads = [jnp.where(lane_head == h, xd_c, zero) for h in range(N_HEADS)]
        st = st_scr[...]
        lhs = jnp.concatenate(scores + c_scaled, axis=1)
        rhs = jnp.concatenate(xd_heads + [st.astype(BF16)], axis=0)
        y = jnp.dot(lhs, rhs, preferred_element_type=F32)
        inc = jnp.dot(jnp.concatenate(b_decayed, axis=0), xd_c, preferred_element_type=F32)
        decay = jnp.concatenate([jnp.broadcast_to(jnp.tile(e, (1, GROUP_W // LANES)), (n, GROUP_W))
                                 for e in e_last], axis=0)
        st_scr[...] = st * decay + jnp.where(state_block, inc, 0.0)

        y = (y + xs[r] * dskip_ref[...]) * gate[r]
        for g in range(SSM_GROUPS):
            gs = slice(g * LANES, (g + 1) * LANES)
            yg = y[:, gs]
            ms = jnp.mean(yg * yg, axis=-1, keepdims=True)
            o_ref[r, gs] = (yg * lax.rsqrt(ms + EPS) * ng_ref[:, gs]).astype(BF16)


def _ssd(ssd_in, ssd_bt, small, dt_rows, dtb, dtb_col, a_neg_col, seldt, d_skip, norm_g, utri_chunks, batch, seq):
    nt = seq // MIX_T
    row = lambda b, t: (b * nt + t, 0)
    return pl.pallas_call(
        _ssd_body,
        grid=(batch, nt),
        in_specs=[pl.BlockSpec((MIX_T, GROUP_W + SSM_XBC), row),
                  pl.BlockSpec((GROUP_W, MIX_T), lambda b, t: (0, b * nt + t)),
                  pl.BlockSpec((MIX_T, LANES), row),
                  pl.BlockSpec((SUBLANES, MIX_T), lambda b, t: (0, b * nt + t)),
                  _const_spec((1, LANES)),
                  _const_spec((SUBLANES, 1)),
                  _const_spec((SUBLANES, 1)),
                  _const_spec((LANES, GROUP_W)),
                  _const_spec((1, GROUP_W)),
                  _const_spec((1, GROUP_W)),
                  _const_spec((MIX_T, MIX_T))],
        out_specs=pl.BlockSpec((MIX_T, GROUP_W), row),
        out_shape=jax.ShapeDtypeStruct((batch * seq, GROUP_W), BF16),
        scratch_shapes=[pltpu.VMEM((N_HEADS * SSM_STATE, GROUP_W), F32)],
        compiler_params=_params(2),
        name="ssd",
    )(ssd_in, ssd_bt, small, dt_rows, dtb, dtb_col, a_neg_col, seldt, d_skip, norm_g, utri_chunks)


def _block_diag_ones(size, block):
    idx = np.arange(size) // block
    return jnp.asarray(idx[:, None] == idx[None, :], BF16)


def _chunk_lower_tri(size, block):
    i = np.arange(size)
    same = (i[:, None] // block) == (i[None, :] // block)
    return jnp.asarray(same & (i[None, :] <= i[:, None]), BF16)


def _lane_spread(src_lane0, width):
    m = np.zeros((LANES, N_HEADS * width))
    for h in range(N_HEADS):
        m[src_lane0 + h, h * width:(h + 1) * width] = 1.0
    return jnp.asarray(m, BF16)


def _row(v):
    return v.reshape(1, -1).astype(F32)


def _lane_slot(values, offset, width):
    return jnp.zeros((1, width), F32).at[0, offset:offset + values.shape[0]].set(values.astype(F32))


def _sublane_col(values):
    return jnp.zeros((SUBLANES, 1), F32).at[0:values.shape[0], 0].set(values.astype(F32))


def _regroup_w_in(w_in):
    gla_end = 4 * GROUP_W
    fox0 = GLA_COLS
    fox_end = fox0 + 3 * GROUP_W
    ssm0 = GLA_COLS + FOX_COLS
    ssm_end = ssm0 + GROUP_W + SSM_XBC
    sc0 = GLA_COLS + FOX_COLS + SSM_COLS
    w_main = jnp.concatenate([w_in[:, 0:gla_end], w_in[:, fox0:fox_end],
                              w_in[:, ssm0:ssm_end], w_in[:, sc0:]], axis=1).astype(BF16)
    w_small = jnp.zeros((D_MODEL, LANES), F32)
    w_small = w_small.at[:, SMALL_LR:SMALL_LR + GLA_GATE_RANK].set(w_in[:, gla_end:GLA_COLS])
    w_small = w_small.at[:, SMALL_F:SMALL_F + N_HEADS].set(w_in[:, fox_end:ssm0])
    w_small = w_small.at[:, SMALL_DT:SMALL_DT + N_HEADS].set(w_in[:, ssm_end:sc0])
    return w_main, w_small.astype(BF16)


def kernel(x, ffn1_norm, ffn1_w_gate, ffn1_w_up, ffn1_w_down, mix_norm, w_in, gla_w_gate_up, gla_b_gate, gla_norm, fox_b_forget, fox_q_norm, fox_k_norm, fox_out_norm, ssm_conv_w, ssm_conv_b, ssm_dt_bias, ssm_A_log, ssm_D, ssm_norm, sc_conv_w, sc_out_norm, w_out, ffn2_norm, ffn2_w_gate, ffn2_w_up, ffn2_w_down):
    batch, seq, _ = x.shape
    depth = w_in.shape[0]
    assert seq % MIX_T == 0 and seq % PROJ_TM == 0 and seq % ATT_TK == 0 and seq % ATT_TQ == 0
    assert (batch * seq) % FFN_TM == 0

    e64 = _block_diag_ones(GROUP_W, HEAD_DIM)
    lb_gla = _chunk_lower_tri(MIX_T, GLA_CHUNK)
    utri_t = jnp.asarray(np.triu(np.ones((PROJ_TM, PROJ_TM))), BF16)
    ub_ssd = _chunk_lower_tri(MIX_T, SSM_CHUNK).T
    seldt = _lane_spread(SMALL_DT, HEAD_DIM)

    w1g, w1u, w1d = _to_bf16(ffn1_w_gate), _to_bf16(ffn1_w_up), _to_bf16(ffn1_w_down)
    w2g, w2u, w2d = _to_bf16(ffn2_w_gate), _to_bf16(ffn2_w_up), _to_bf16(ffn2_w_down)
    wo = _to_bf16(w_out)

    xf = x.reshape(batch * seq, D_MODEL)
    for l in range(depth):
        xf = _ffn(xf, _row(ffn1_norm[l]), w1g[l], w1u[l], w1d[l])

        w_main, w_small = _regroup_w_in(w_in[l])
        gla_in, gla_vt, qn, kn, vf, f_cum, ssd_in, ssd_bt, yd, small, dt_rows = _inproj(
            xf, _row(mix_norm[l]), w_main, w_small, _sublane_col(fox_b_forget[l]),
            _row(jnp.tile(fox_q_norm[l], N_HEADS)), _row(jnp.tile(fox_k_norm[l], N_HEADS)),
            e64, utri_t, ssm_conv_w[l].astype(F32), _row(ssm_conv_b[l]),
            sc_conv_w[l].astype(F32), _row(sc_out_norm[l]), seq)

        wup_pad = jnp.zeros((LANES, GROUP_W), F32).at[SMALL_LR:SMALL_LR + GLA_GATE_RANK].set(
            gla_w_gate_up[l]).astype(BF16)
        ya = _gla(gla_in, gla_vt, small, wup_pad, _row(gla_b_gate[l]), _row(gla_norm[l]), lb_gla, e64, batch, seq)

        yb = _attn(qn, kn, vf, f_cum, _row(fox_out_norm[l]), batch, seq)

        a_neg = -jnp.exp(ssm_A_log[l].astype(F32))
        yc = _ssd(ssd_in, ssd_bt, small, dt_rows, _lane_slot(ssm_dt_bias[l], SMALL_DT, LANES),
                  _sublane_col(ssm_dt_bias[l]), _sublane_col(a_neg), seldt,
                  _row(jnp.repeat(ssm_D[l], HEAD_DIM)), _row(ssm_norm[l]), ub_ssd, batch, seq)

        xf = _ffn(xf, _row(ffn2_norm[l]), w2g[l], w2u[l], w2d[l], mixers=(ya, yb, yc, yd), w_out=wo[l])
    return xf.reshape(batch, seq, D_MODEL)
```

```python
import functools

import numpy as np
import jax
import jax.numpy as jnp
from jax import lax
from jax.experimental import pallas as pl
from jax.experimental.pallas import tpu as pltpu

F32, BF16 = jnp.float32, jnp.bfloat16

D_MODEL = 1024
GROUP_W = 256
HEAD_DIM = 64
N_HEADS = GROUP_W // HEAD_DIM
GLA_GATE_RANK = 16
GLA_GATE_NORMALIZER = 16.0
GLA_CHUNK = 64
SSM_GROUPS = 2
SSM_STATE = 128
SSM_CONV = 4
SSM_CHUNK = 128
SSM_XBC = GROUP_W + 2 * SSM_GROUPS * SSM_STATE
SC_GROUPS = 4
SC_CONV = 3
D_FF = 2816
EPS = 1e-6
LOG2_E = 1.4426950408889634

GLA_COLS = 4 * GROUP_W + GLA_GATE_RANK
FOX_COLS = 3 * GROUP_W + N_HEADS
SSM_COLS = GROUP_W + SSM_XBC + N_HEADS

COL_GLA = 0
COL_FOX = COL_GLA + 4 * GROUP_W
COL_SSD = COL_FOX + 3 * GROUP_W
COL_SC = COL_SSD + GROUP_W + SSM_XBC
MAIN_COLS = COL_SC + 3 * GROUP_W

LANES = 128
SUBLANES = 8
SMALL_LR = 0
SMALL_F = 16
SMALL_DT = 24

FFN_TM = 512
FF_CHUNK = 256
PROJ_TM = 512
MIX_T = 512
ATT_TQ = 512
ATT_TK = 512
CAST_ROW_BLOCKS = 4
HALO = 8

VMEM_LIMIT = 56 * 1024 * 1024


def _const_spec(shape):
    nd = len(shape)
    return pl.BlockSpec(shape, lambda *_: (0,) * nd, pipeline_mode=pl.Buffered(1))


def _params(n_axes):
    return pltpu.CompilerParams(dimension_semantics=("arbitrary",) * n_axes,
                                vmem_limit_bytes=VMEM_LIMIT)


def _split_bf16(x, pieces):
    out, r = [], x
    for i in range(pieces):
        p = r.astype(BF16)
        out.append(p)
        if i + 1 < pieces:
            r = r - p.astype(F32)
    return out


def _dot_x01(x, m01, pieces):
    acc = None
    for p in _split_bf16(x, pieces):
        d = jnp.dot(p, m01, preferred_element_type=F32)
        acc = d if acc is None else acc + d
    return acc


def _dot_01x(m01, x, pieces):
    acc = None
    for p in _split_bf16(x, pieces):
        d = jnp.dot(m01, p, preferred_element_type=F32)
        acc = d if acc is None else acc + d
    return acc


def _dot_nt(a, b):
    return lax.dot_general(a, b, (((1,), (1,)), ((), ())), preferred_element_type=F32)


def _log_sigmoid(x):
    return jnp.minimum(x, 0.0) - jnp.log1p(jnp.exp(-jnp.abs(x)))


def _softplus(x):
    return jnp.maximum(x, 0.0) + jnp.log1p(jnp.exp(-jnp.abs(x)))


def _silu(x):
    return x * jax.nn.sigmoid(x)


def _rms_rows(x, gain):
    ms = jnp.mean(x * x, axis=-1, keepdims=True)
    return x * lax.rsqrt(ms + EPS) * gain


def _group_rms(x, ones_blocks, width, gain):
    ssq = _dot_x01(x * x, ones_blocks, 1)
    return x * lax.rsqrt(ssq * (1.0 / width) + EPS) * gain


def _head_id(shape, axis, width):
    return lax.broadcasted_iota(jnp.int32, shape, axis) // width


def _swiglu_half_step(x, g_ref, wg_ref, wu_ref, wd_ref, a_scr):
    h = _rms_rows(x, g_ref[...]).astype(BF16)
    for j in range(D_FF // FF_CHUNK):
        sl = slice(j * FF_CHUNK, (j + 1) * FF_CHUNK)
        gate = jnp.dot(h, wg_ref[:, sl], preferred_element_type=F32)
        up = jnp.dot(h, wu_ref[:, sl], preferred_element_type=F32)
        a_scr[:, sl] = (_silu(gate) * up).astype(BF16)
    return x + 0.5 * jnp.dot(a_scr[...], wd_ref[...], preferred_element_type=F32)


def _ffn_body(x_ref, g_ref, wg_ref, wu_ref, wd_ref, o_ref, a_scr):
    o_ref[...] = _swiglu_half_step(x_ref[...], g_ref, wg_ref, wu_ref, wd_ref, a_scr)


def _mix_ffn_body(x_ref, ya_ref, yb_ref, yc_ref, yd_ref, wo_ref, g_ref, wg_ref, wu_ref, wd_ref, o_ref, a_scr):
    x = x_ref[...]
    for i, y_ref in enumerate((ya_ref, yb_ref, yc_ref, yd_ref)):
        x = x + jnp.dot(y_ref[...], wo_ref[i * GROUP_W:(i + 1) * GROUP_W, :], preferred_element_type=F32)
    o_ref[...] = _swiglu_half_step(x, g_ref, wg_ref, wu_ref, wd_ref, a_scr)


def _ffn(x, gain, wg, wu, wd, mixers=None, w_out=None):
    m = x.shape[0]
    row = lambda i: (i, 0)
    ffn_specs = [_const_spec((1, D_MODEL)),
                 _const_spec((D_MODEL, D_FF)),
                 _const_spec((D_MODEL, D_FF)),
                 _const_spec((D_FF, D_MODEL))]
    if mixers is None:
        body, name, lead, lead_specs = _ffn_body, "ffn", (), []
    else:
        body, name, lead = _mix_ffn_body, "mix_ffn", tuple(mixers) + (w_out,)
        lead_specs = [pl.BlockSpec((FFN_TM, GROUP_W), row)] * len(mixers) + [_const_spec((4 * GROUP_W, D_MODEL))]
    return pl.pallas_call(
        body,
        grid=(m // FFN_TM,),
        in_specs=[pl.BlockSpec((FFN_TM, D_MODEL), row)] + lead_specs + ffn_specs,
        out_specs=pl.BlockSpec((FFN_TM, D_MODEL), row),
        out_shape=jax.ShapeDtypeStruct((m, D_MODEL), F32),
        scratch_shapes=[pltpu.VMEM((FFN_TM, D_FF), BF16)],
        compiler_params=_params(1),
        name=name,
    )(x, *lead, gain, wg, wu, wd)


def _cast_body(w_ref, o_ref):
    o_ref[...] = w_ref[...].astype(BF16)


def _to_bf16(w):
    depth, rows, cols = w.shape
    block_rows = rows // CAST_ROW_BLOCKS
    assert block_rows * CAST_ROW_BLOCKS == rows and block_rows % 16 == 0
    spec = pl.BlockSpec((1, block_rows, cols), lambda l, i: (l, i, 0))
    return pl.pallas_call(
        _cast_body,
        grid=(depth, CAST_ROW_BLOCKS),
        in_specs=[spec],
        out_specs=spec,
        out_shape=jax.ShapeDtypeStruct(w.shape, BF16),
        compiler_params=_params(2),
        name="cast_bf16",
    )(w)


def _inproj_body(x_ref, g_ref, wm_ref, ws_ref, bf_ref, qg_ref, kg_ref, e_ref, ut_ref,
                 cw_ref, cb_ref, scw_ref, scg_ref,
                 gla_ref, gvt_ref, qn_ref, kn_ref, vf_ref, f_ref, ssd_ref, sbt_ref, yd_ref, small_ref, dtr_ref,
                 xb_scr, u_scr, f_carry, *, tiles_per_seq):
    t = PROJ_TM

    @pl.when(pl.program_id(0) % tiles_per_seq == 0)
    def _():
        xb_scr[0:HALO, :] = jnp.zeros((HALO, SSM_XBC), F32)
        u_scr[0:HALO, :] = jnp.zeros((HALO, GROUP_W), F32)
        f_carry[...] = jnp.zeros_like(f_carry)

    h = _rms_rows(x_ref[...], g_ref[...]).astype(BF16)

    def proj(col):
        return jnp.dot(h, wm_ref[:, col:col + GROUP_W], preferred_element_type=F32)

    n_xbc = SSM_XBC // GROUP_W
    p_xbc = [proj(COL_SSD + GROUP_W + c * GROUP_W) for c in range(n_xbc)]
    p_sc = [proj(COL_SC + c * GROUP_W) for c in range(3)]
    p_fox = [proj(COL_FOX + c * GROUP_W) for c in range(3)]
    small = jnp.dot(h, ws_ref[...], preferred_element_type=F32)
    p_z = proj(COL_SSD)
    p_gla = [proj(COL_GLA + c * GROUP_W) for c in range(4)]

    for c in range(4):
        gla_ref[:, c * GROUP_W:(c + 1) * GROUP_W] = p_gla[c].astype(BF16)
    gvt_ref[...] = p_gla[2].T.astype(BF16)

    vf_ref[...] = p_fox[2].astype(BF16)
    q_scale = HEAD_DIM ** -0.5 * LOG2_E

    ssd_ref[:, 0:GROUP_W] = p_z.astype(BF16)
    for c in range(n_xbc):
        cols = slice(c * GROUP_W, (c + 1) * GROUP_W)
        xb_scr[HALO:HALO + t, cols] = p_xbc[c]
    for c in range(n_xbc):
        cols = slice(c * GROUP_W, (c + 1) * GROUP_W)
        conv = cb_ref[:, cols]
        for kk in range(SSM_CONV):
            conv = conv + cw_ref[kk:kk + 1, cols] * xb_scr[pl.ds(HALO - (SSM_CONV - 1) + kk, t), cols]
        act = _silu(conv)
        ssd_ref[:, GROUP_W + c * GROUP_W:2 * GROUP_W + c * GROUP_W] = act.astype(BF16)
        if c == 1:
            sbt_ref[...] = act.T.astype(BF16)
    xb_scr[0:HALO, :] = xb_scr[t:t + HALO, :]

    u_scr[HALO:HALO + t, :] = p_sc[1] * p_sc[2]
    conv = jnp.zeros((t, GROUP_W), F32)
    for kk in range(SC_CONV):
        conv = conv + scw_ref[kk:kk + 1, :] * u_scr[pl.ds(HALO - (SC_CONV - 1) + kk, t), :]
    u_scr[0:HALO, :] = u_scr[t:t + HALO, :]

    qn_ref[...] = (_group_rms(p_fox[0], e_ref[...], HEAD_DIM, qg_ref[...]) * q_scale).astype(BF16)
    kn_ref[...] = _group_rms(p_fox[1], e_ref[...], HEAD_DIM, kg_ref[...]).astype(BF16)
    yd_ref[...] = _group_rms(p_sc[0] * conv, e_ref[...], GROUP_W // SC_GROUPS, scg_ref[...]).astype(BF16)

    small_ref[...] = small
    small_t = small.T
    dtr_ref[...] = small_t[SMALL_DT:SMALL_DT + SUBLANES, :]
    log_f = _log_sigmoid(small_t[SMALL_F:SMALL_F + SUBLANES, :] + bf_ref[...])
    parts = jnp.dot(jnp.concatenate(_split_bf16(log_f, 3), axis=0), ut_ref[...], preferred_element_type=F32)
    f_ref[...] = (parts[0:SUBLANES] + parts[SUBLANES:2 * SUBLANES] + parts[2 * SUBLANES:3 * SUBLANES]
                  + f_carry[:, 0:1])
    f_carry[...] = f_carry[...] + jnp.sum(log_f, axis=-1, keepdims=True)


def _inproj(x, gain, w_main, w_small, b_forget, q_gain, k_gain, e64, utri, conv_w, conv_b,
            sc_conv_w, sc_gain, seq):
    m = x.shape[0]
    t = PROJ_TM
    row = lambda i: (i, 0)
    col = lambda i: (0, i)
    out_shape = [jax.ShapeDtypeStruct((m, 4 * GROUP_W), BF16),
                 jax.ShapeDtypeStruct((GROUP_W, m), BF16),
                 jax.ShapeDtypeStruct((m, GROUP_W), BF16),
                 jax.ShapeDtypeStruct((m, GROUP_W), BF16),
                 jax.ShapeDtypeStruct((m, GROUP_W), BF16),
                 jax.ShapeDtypeStruct((SUBLANES, m), F32),
                 jax.ShapeDtypeStruct((m, GROUP_W + SSM_XBC), BF16),
                 jax.ShapeDtypeStruct((GROUP_W, m), BF16),
                 jax.ShapeDtypeStruct((m, GROUP_W), BF16),
                 jax.ShapeDtypeStruct((m, LANES), F32),
                 jax.ShapeDtypeStruct((SUBLANES, m), F32)]
    out_specs = [pl.BlockSpec((t, 4 * GROUP_W), row),
                 pl.BlockSpec((GROUP_W, t), col),
                 pl.BlockSpec((t, GROUP_W), row),
                 pl.BlockSpec((t, GROUP_W), row),
                 pl.BlockSpec((t, GROUP_W), row),
                 pl.BlockSpec((SUBLANES, t), col),
                 pl.BlockSpec((t, GROUP_W + SSM_XBC), row),
                 pl.BlockSpec((GROUP_W, t), col),
                 pl.BlockSpec((t, GROUP_W), row),
                 pl.BlockSpec((t, LANES), row),
                 pl.BlockSpec((SUBLANES, t), col)]
    return pl.pallas_call(
        functools.partial(_inproj_body, tiles_per_seq=seq // t),
        grid=(m // t,),
        in_specs=[pl.BlockSpec((t, D_MODEL), row),
                  _const_spec((1, D_MODEL)),
                  _const_spec((D_MODEL, MAIN_COLS)),
                  _const_spec((D_MODEL, LANES)),
                  _const_spec((SUBLANES, 1)),
                  _const_spec((1, GROUP_W)),
                  _const_spec((1, GROUP_W)),
                  _const_spec((GROUP_W, GROUP_W)),
                  _const_spec((t, t)),
                  _const_spec((SSM_CONV, SSM_XBC)),
                  _const_spec((1, SSM_XBC)),
                  _const_spec((SC_CONV, GROUP_W)),
                  _const_spec((1, GROUP_W))],
        out_specs=out_specs,
        out_shape=out_shape,
        scratch_shapes=[pltpu.VMEM((t + HALO, SSM_XBC), F32),
                        pltpu.VMEM((t + HALO, GROUP_W), F32),
                        pltpu.VMEM((SUBLANES, LANES), F32)],
        compiler_params=_params(1),
        name="inproj",
    )(x, gain, w_main, w_small, b_forget, q_gain, k_gain, e64, utri, conv_w, conv_b, sc_conv_w, sc_gain)


def _gla_body(x_ref, vt_ref, sm_ref, wup_ref, bg_ref, ng_ref, lb_ref, e_ref, o_ref, st_scr, o_scr):
    t = MIX_T

    @pl.when(pl.program_id(1) == 0)
    def _():
        st_scr[...] = jnp.zeros_like(st_scr)

    q = x_ref[:, 0:GROUP_W].astype(F32) * (HEAD_DIM ** -0.5)
    k = x_ref[:, GROUP_W:2 * GROUP_W].astype(F32)
    v = x_ref[:, 2 * GROUP_W:3 * GROUP_W]
    g_out = x_ref[:, 3 * GROUP_W:4 * GROUP_W].astype(F32)

    z = jnp.dot(sm_ref[...].astype(BF16), wup_ref[...], preferred_element_type=F32) + bg_ref[...]
    log_a = _log_sigmoid(z) * (1.0 / GLA_GATE_NORMALIZER)
    c = GLA_CHUNK
    head_rows = _head_id((GROUP_W, GROUP_W), 0, HEAD_DIM)
    head_cols = _head_id((GROUP_W, GROUP_W), 1, HEAD_DIM)
    block_diag = head_rows == head_cols
    lane_head = _head_id((c, GROUP_W), 1, HEAD_DIM)
    row_id = lax.broadcasted_iota(jnp.int32, (c, GROUP_W), 0)
    key_id = lax.broadcasted_iota(jnp.int32, (c, GROUP_W), 1) % HEAD_DIM
    causal = key_id <= row_id

    n_chunks = t // c
    rows = [slice(ci * c, (ci + 1) * c) for ci in range(n_chunks)]
    b_chunks = [_dot_01x(lb_ref[...], log_a[r], 3) for r in rows]
    qd, ke, v_stack, att, decay = [], [], [], [], []
    for r, bc in zip(rows, b_chunks):
        bl = bc[c - 1:c, :]
        decay.append(jnp.exp(bl))
        qd.append((q[r] * jnp.exp(bc)).astype(BF16))
        kd = (k[r] * jnp.exp(-bc)).astype(BF16)
        ke.append((k[r] * jnp.exp(bl - bc)).astype(BF16))
        zero = jnp.zeros_like(kd)
        k_stack = jnp.concatenate([jnp.where(lane_head == h, kd, zero) for h in range(N_HEADS)], axis=0)
        v_stack.append(jnp.concatenate([jnp.where(lane_head == h, v[r], zero) for h in range(N_HEADS)], axis=0))
        att.append(_dot_nt(qd[-1], k_stack))

    kv_t = []
    for ci in range(n_chunks):
        half = jnp.zeros_like(ke[ci])
        ke_pair = jnp.concatenate([ke[ci], half] if ci % 2 == 0 else [half, ke[ci]], axis=0)
        kv_t.append(jnp.dot(vt_ref[:, (ci // 2) * LANES:(ci // 2 + 1) * LANES], ke_pair,
                            preferred_element_type=F32))

    st = st_scr[...]
    for ci in range(n_chunks):
        att_c = jnp.where(causal, att[ci], 0.0).astype(BF16)
        o_scr[rows[ci], :] = (jnp.dot(att_c, v_stack[ci], preferred_element_type=F32)
                              + _dot_nt(qd[ci], st.astype(BF16)))
        st = st * decay[ci] + jnp.where(block_diag, kv_t[ci], 0.0)
    st_scr[...] = st

    y = _group_rms(o_scr[...], e_ref[...], HEAD_DIM, ng_ref[...]) * _silu(g_out)
    o_ref[...] = y.astype(BF16)


def _gla(gla_in, gla_vt, small, wup_pad, b_gate, norm_g, lb, e64, batch, seq):
    nt = seq // MIX_T
    row = lambda b, t: (b * nt + t, 0)
    return pl.pallas_call(
        _gla_body,
        grid=(batch, nt),
        in_specs=[pl.BlockSpec((MIX_T, 4 * GROUP_W), row),
                  pl.BlockSpec((GROUP_W, MIX_T), lambda b, t: (0, b * nt + t)),
                  pl.BlockSpec((MIX_T, LANES), row),
                  _const_spec((LANES, GROUP_W)),
                  _const_spec((1, GROUP_W)),
                  _const_spec((1, GROUP_W)),
                  _const_spec((GLA_CHUNK, GLA_CHUNK)),
                  _const_spec((GROUP_W, GROUP_W))],
        out_specs=pl.BlockSpec((MIX_T, GROUP_W), row),
        out_shape=jax.ShapeDtypeStruct((batch * seq, GROUP_W), BF16),
        scratch_shapes=[pltpu.VMEM((GROUP_W, GROUP_W), F32),
                        pltpu.VMEM((MIX_T, GROUP_W), F32)],
        compiler_params=_params(2),
        name="gla",
    )(gla_in, gla_vt, small, wup_pad, b_gate, norm_g, lb, e64)


def _attn_body(q_ref, k_ref, v_ref, f_ref, og_ref, o_ref, qs_scr, m_scr, acc_scr, p_scr, al_scr):
    tq, tk = ATT_TQ, ATT_TK
    i = pl.program_id(1)
    q0 = pl.multiple_of(i * tq, tq)

    q = q_ref[...]
    lane_head = _head_id((tq, GROUP_W), 1, HEAD_DIM)
    for h in range(N_HEADS):
        qs_scr[h * tq:(h + 1) * tq, :] = jnp.where(lane_head == h, q, jnp.zeros_like(q))
    m_scr[...] = jnp.full_like(m_scr, -jnp.inf)
    acc_scr[...] = jnp.zeros_like(acc_scr)

    f_here = f_ref[:, pl.ds(q0, LANES)][:, 0:1]
    key_lane_head = _head_id((tk, GROUP_W), 1, HEAD_DIM)

    def scores_to_probs(j, masked):
        c0 = pl.multiple_of(j * tk, tk)
        k = k_ref[pl.ds(c0, tk), :]
        bias = (f_here - f_ref[:, pl.ds(c0, tk)]) * LOG2_E
        if masked:
            keep = (lax.broadcasted_iota(jnp.int32, (tq, tk), 1) + c0
                    <= lax.broadcasted_iota(jnp.int32, (tq, tk), 0) + q0)
        for h in range(N_HEADS):
            hs = slice(h * tq, (h + 1) * tq)
            s = _dot_nt(qs_scr[hs, :], k) + bias[h:h + 1, :]
            if masked:
                s = jnp.where(keep, s, -jnp.inf)
            m_prev = m_scr[hs]
            m_new = jnp.maximum(m_prev, jnp.max(s, axis=-1, keepdims=True))
            al_scr[hs] = jnp.exp2(m_prev - m_new)
            p_scr[hs, :] = jnp.exp2(s - jnp.tile(m_new, (1, tk // LANES))).astype(BF16)
            m_scr[hs] = m_new

    def accumulate(j):
        c0 = pl.multiple_of(j * tk, tk)
        v = v_ref[pl.ds(c0, tk), :]
        for h in range(N_HEADS):
            hs = slice(h * tq, (h + 1) * tq)
            v_h = jnp.where(key_lane_head == (h + 1) % N_HEADS, jnp.ones_like(v), v)
            acc_scr[hs, :] = (acc_scr[hs, :] * jnp.tile(al_scr[hs], (1, GROUP_W // LANES))
                              + jnp.dot(p_scr[hs, :], v_h, preferred_element_type=F32))

    n_full = (i * tq) // tk

    scores_to_probs(0, True)

    def loop_body(j, carry):
        accumulate(j - 1)
        scores_to_probs(j, False)
        return carry

    lax.fori_loop(1, n_full, loop_body, 0)

    @pl.when(n_full >= 1)
    def _():
        accumulate(n_full - 1)
        scores_to_probs(n_full, True)

    accumulate(n_full)

    y = jnp.zeros((tq, GROUP_W), F32)
    for h in range(N_HEADS):
        acc = acc_scr[h * tq:(h + 1) * tq, :]
        row_sum = jnp.max(jnp.where(lane_head == (h + 1) % N_HEADS, acc, -jnp.inf), axis=-1, keepdims=True)
        oh = acc / row_sum
        mine = lane_head == h
        ssq = jnp.sum(jnp.where(mine, oh * oh, 0.0), axis=-1, keepdims=True)
        y = jnp.where(mine, oh * lax.rsqrt(ssq * (1.0 / HEAD_DIM) + EPS), y)
    o_ref[...] = (y * og_ref[...]).astype(BF16)


def _attn(qn, kn, vf, f_cum, out_gain, batch, seq):
    assert ATT_TQ == ATT_TK, "the key-tile pipeline assumes exactly one tile on the causal diagonal"
    nq = seq // ATT_TQ
    rows = N_HEADS * ATT_TQ
    return pl.pallas_call(
        _attn_body,
        grid=(batch, nq),
        in_specs=[pl.BlockSpec((ATT_TQ, GROUP_W), lambda b, i: (b * nq + i, 0)),
                  pl.BlockSpec((seq, GROUP_W), lambda b, i: (b, 0)),
                  pl.BlockSpec((seq, GROUP_W), lambda b, i: (b, 0)),
                  pl.BlockSpec((SUBLANES, seq), lambda b, i: (0, b)),
                  _const_spec((1, GROUP_W))],
        out_specs=pl.BlockSpec((ATT_TQ, GROUP_W), lambda b, i: (b * nq + i, 0)),
        out_shape=jax.ShapeDtypeStruct((batch * seq, GROUP_W), BF16),
        scratch_shapes=[pltpu.VMEM((rows, GROUP_W), BF16),
                        pltpu.VMEM((rows, LANES), F32),
                        pltpu.VMEM((rows, GROUP_W), F32),
                        pltpu.VMEM((rows, ATT_TK), BF16),
                        pltpu.VMEM((rows, LANES), F32)],
        compiler_params=_params(2),
        name="fox_attn",
    )(qn, kn, vf, f_cum, out_gain)


def _ssd_body(x_ref, bt_ref, sm_ref, dtr_ref, dtb_ref, dtbr_ref, anr_ref, seldt_ref, dskip_ref, ng_ref, ub_ref,
              o_ref, st_scr):
    t, qc, n = MIX_T, SSM_CHUNK, SSM_STATE

    @pl.when(pl.program_id(1) == 0)
    def _():
        st_scr[...] = jnp.zeros_like(st_scr)

    gate = _silu(x_ref[:, 0:GROUP_W].astype(F32))
    xs = x_ref[:, GROUP_W:2 * GROUP_W].astype(F32)
    dt_col = _softplus(sm_ref[...] + dtb_ref[...])
    xd = (xs * _dot_x01(dt_col, seldt_ref[...], 2)).astype(BF16)

    a_row = _softplus(dtr_ref[...] + dtbr_ref[...]) * anr_ref[...]
    pieces = jnp.dot(jnp.concatenate(_split_bf16(a_row, 3), axis=0), ub_ref[...], preferred_element_type=F32)
    acs_row = pieces[0:SUBLANES] + pieces[SUBLANES:2 * SUBLANES] + pieces[2 * SUBLANES:3 * SUBLANES]

    lane_head = _head_id((qc, GROUP_W), 1, HEAD_DIM)
    tril = (lax.broadcasted_iota(jnp.int32, (qc, qc), 1)
            <= lax.broadcasted_iota(jnp.int32, (qc, qc), 0))
    heads_per_group = N_HEADS // SSM_GROUPS
    state_block = (_head_id((N_HEADS * n, GROUP_W), 0, n) == _head_id((N_HEADS * n, GROUP_W), 1, HEAD_DIM))

    n_chunks = t // qc
    rows = [slice(ci * qc, (ci + 1) * qc) for ci in range(n_chunks)]
    bt_grp = [[bt_ref[g * n:(g + 1) * n, r] for g in range(SSM_GROUPS)] for r in rows]
    c_grp = [[x_ref[r, 2 * GROUP_W + (SSM_GROUPS + g) * n:2 * GROUP_W + (SSM_GROUPS + g + 1) * n]
              for g in range(SSM_GROUPS)] for r in rows]
    cb = [[jnp.dot(c_grp[ci][g], bt_grp[ci][g], preferred_element_type=F32) for g in range(SSM_GROUPS)]
          for ci in range(n_chunks)]

    lhs_parts, decays, incs = [], [], []
    for ci, r in enumerate(rows):
        scores, c_scaled, b_decayed, e_last = [], [], [], []
        for h in range(N_HEADS):
            g = h // heads_per_group
            ac_t = jnp.broadcast_to(acs_row[h:h + 1, r], (qc, qc))
            ac = ac_t.T
            lmat = jnp.exp(jnp.where(tril, ac - ac_t, -jnp.inf))
            scores.append((cb[ci][g] * lmat).astype(BF16))
            a_last = ac[qc - 1:qc, :]
            e_last.append(jnp.exp(a_last))
            b_decayed.append((bt_grp[ci][g].astype(F32) * jnp.exp(a_last - ac_t[0:1, :])).astype(BF16))
            c_scaled.append((c_grp[ci][g].astype(F32) * jnp.exp(ac)).astype(BF16))
        lhs_parts.append(jnp.concatenate(scores + c_scaled, axis=1))
        decays.append(jnp.concatenate([jnp.broadcast_to(jnp.tile(e, (1, GROUP_W // LANES)), (n, GROUP_W))
                                       for e in e_last], axis=0))
        incs.append(jnp.dot(jnp.concatenate(b_decayed, axis=0), xd[r], preferred_element_type=F32))

    st = st_scr[...]
    for ci, r in enumerate(rows):
        xd_c = xd[r]
        zero = jnp.zeros_like(xd_c)
        xd_heads = [jnp.where(lane_head == h, xd_c, zero) for h in range(N_HEADS)]
        rhs = jnp.concatenate(xd_heads + [st.astype(BF16)], axis=0)
        y = jnp.dot(lhs_parts[ci], rhs, preferred_element_type=F32)
        st = st * decays[ci] + jnp.where(state_block, incs[ci], 0.0)

        y = (y + xs[r] * dskip_ref[...]) * gate[r]
        for g in range(SSM_GROUPS):
            gs = slice(g * LANES, (g + 1) * LANES)
            yg = y[:, gs]
            ms = jnp.mean(yg * yg, axis=-1, keepdims=True)
            o_ref[r, gs] = (yg * lax.rsqrt(ms + EPS) * ng_ref[:, gs]).astype(BF16)
    st_scr[...] = st


def _ssd(ssd_in, ssd_bt, small, dt_rows, dtb, dtb_col, a_neg_col, seldt, d_skip, norm_g, utri_chunks, batch, seq):
    nt = seq // MIX_T
    row = lambda b, t: (b * nt + t, 0)
    return pl.pallas_call(
        _ssd_body,
        grid=(batch, nt),
        in_specs=[pl.BlockSpec((MIX_T, GROUP_W + SSM_XBC), row),
                  pl.BlockSpec((GROUP_W, MIX_T), lambda b, t: (0, b * nt + t)),
                  pl.BlockSpec((MIX_T, LANES), row),
                  pl.BlockSpec((SUBLANES, MIX_T), lambda b, t: (0, b * nt + t)),
                  _const_spec((1, LANES)),
                  _const_spec((SUBLANES, 1)),
                  _const_spec((SUBLANES, 1)),
                  _const_spec((LANES, GROUP_W)),
                  _const_spec((1, GROUP_W)),
                  _const_spec((1, GROUP_W)),
                  _const_spec((MIX_T, MIX_T))],
        out_specs=pl.BlockSpec((MIX_T, GROUP_W), row),
        out_shape=jax.ShapeDtypeStruct((batch * seq, GROUP_W), BF16),
        scratch_shapes=[pltpu.VMEM((N_HEADS * SSM_STATE, GROUP_W), F32)],
        compiler_params=_params(2),
        name="ssd",
    )(ssd_in, ssd_bt, small, dt_rows, dtb, dtb_col, a_neg_col, seldt, d_skip, norm_g, utri_chunks)


def _block_diag_ones(size, block):
    idx = np.arange(size) // block
    return jnp.asarray(idx[:, None] == idx[None, :], BF16)


def _chunk_lower_tri(size, block):
    i = np.arange(size)
    same = (i[:, None] // block) == (i[None, :] // block)
    return jnp.asarray(same & (i[None, :] <= i[:, None]), BF16)


def _lane_spread(src_lane0, width):
    m = np.zeros((LANES, N_HEADS * width))
    for h in range(N_HEADS):
        m[src_lane0 + h, h * width:(h + 1) * width] = 1.0
    return jnp.asarray(m, BF16)


def _row(v):
    return v.reshape(1, -1).astype(F32)


def _lane_slot(values, offset, width):
    return jnp.zeros((1, width), F32).at[0, offset:offset + values.shape[0]].set(values.astype(F32))


def _sublane_col(values):
    return jnp.zeros((SUBLANES, 1), F32).at[0:values.shape[0], 0].set(values.astype(F32))


def _regroup_w_in(w_in):
    gla_end = 4 * GROUP_W
    fox0 = GLA_COLS
    fox_end = fox0 + 3 * GROUP_W
    ssm0 = GLA_COLS + FOX_COLS
    ssm_end = ssm0 + GROUP_W + SSM_XBC
    sc0 = GLA_COLS + FOX_COLS + SSM_COLS
    w_main = jnp.concatenate([w_in[:, 0:gla_end], w_in[:, fox0:fox_end],
                              w_in[:, ssm0:ssm_end], w_in[:, sc0:]], axis=1).astype(BF16)
    w_small = jnp.zeros((D_MODEL, LANES), F32)
    w_small = w_small.at[:, SMALL_LR:SMALL_LR + GLA_GATE_RANK].set(w_in[:, gla_end:GLA_COLS])
    w_small = w_small.at[:, SMALL_F:SMALL_F + N_HEADS].set(w_in[:, fox_end:ssm0])
    w_small = w_small.at[:, SMALL_DT:SMALL_DT + N_HEADS].set(w_in[:, ssm_end:sc0])
    return w_main, w_small.astype(BF16)


def kernel(x, ffn1_norm, ffn1_w_gate, ffn1_w_up, ffn1_w_down, mix_norm, w_in, gla_w_gate_up, gla_b_gate, gla_norm, fox_b_forget, fox_q_norm, fox_k_norm, fox_out_norm, ssm_conv_w, ssm_conv_b, ssm_dt_bias, ssm_A_log, ssm_D, ssm_norm, sc_conv_w, sc_out_norm, w_out, ffn2_norm, ffn2_w_gate, ffn2_w_up, ffn2_w_down):
    batch, seq, _ = x.shape
    depth = w_in.shape[0]
    assert seq % MIX_T == 0 and seq % PROJ_TM == 0 and seq % ATT_TK == 0 and seq % ATT_TQ == 0
    assert (batch * seq) % FFN_TM == 0

    e64 = _block_diag_ones(GROUP_W, HEAD_DIM)
    lb_gla = _chunk_lower_tri(GLA_CHUNK, GLA_CHUNK)
    utri_t = jnp.asarray(np.triu(np.ones((PROJ_TM, PROJ_TM))), BF16)
    ub_ssd = _chunk_lower_tri(MIX_T, SSM_CHUNK).T
    seldt = _lane_spread(SMALL_DT, HEAD_DIM)

    w1g, w1u, w1d = _to_bf16(ffn1_w_gate), _to_bf16(ffn1_w_up), _to_bf16(ffn1_w_down)
    w2g, w2u, w2d = _to_bf16(ffn2_w_gate), _to_bf16(ffn2_w_up), _to_bf16(ffn2_w_down)
    wo = _to_bf16(w_out)

    xf = x.reshape(batch * seq, D_MODEL)
    for l in range(depth):
        xf = _ffn(xf, _row(ffn1_norm[l]), w1g[l], w1u[l], w1d[l])

        w_main, w_small = _regroup_w_in(w_in[l])
        gla_in, gla_vt, qn, kn, vf, f_cum, ssd_in, ssd_bt, yd, small, dt_rows = _inproj(
            xf, _row(mix_norm[l]), w_main, w_small, _sublane_col(fox_b_forget[l]),
            _row(jnp.tile(fox_q_norm[l], N_HEADS)), _row(jnp.tile(fox_k_norm[l], N_HEADS)),
            e64, utri_t, ssm_conv_w[l].astype(F32), _row(ssm_conv_b[l]),
            sc_conv_w[l].astype(F32), _row(sc_out_norm[l]), seq)

        wup_pad = jnp.zeros((LANES, GROUP_W), F32).at[SMALL_LR:SMALL_LR + GLA_GATE_RANK].set(
            gla_w_gate_up[l]).astype(BF16)
        ya = _gla(gla_in, gla_vt, small, wup_pad, _row(gla_b_gate[l]), _row(gla_norm[l]), lb_gla, e64, batch, seq)

        yb = _attn(qn, kn, vf, f_cum, _row(fox_out_norm[l]), batch, seq)

        a_neg = -jnp.exp(ssm_A_log[l].astype(F32))
        yc = _ssd(ssd_in, ssd_bt, small, dt_rows, _lane_slot(ssm_dt_bias[l], SMALL_DT, LANES),
                  _sublane_col(ssm_dt_bias[l]), _sublane_col(a_neg), seldt,
                  _row(jnp.repeat(ssm_D[l], HEAD_DIM)), _row(ssm_norm[l]), ub_ssd, batch, seq)

        xf = _ffn(xf, _row(ffn2_norm[l]), w2g[l], w2u[l], w2d[l], mixers=(ya, yb, yc, yd), w_out=wo[l])
    return xf.reshape(batch, seq, D_MODEL)
```

```python
import functools

import numpy as np
import jax
import jax.numpy as jnp
from jax import lax
from jax.experimental import pallas as pl
from jax.experimental.pallas import tpu as pltpu

F32, BF16 = jnp.float32, jnp.bfloat16

D_MODEL = 1024
GROUP_W = 256
HEAD_DIM = 64
N_HEADS = GROUP_W // HEAD_DIM
GLA_GATE_RANK = 16
GLA_GATE_NORMALIZER = 16.0
GLA_CHUNK = 64
SSM_GROUPS = 2
SSM_STATE = 128
SSM_CONV = 4
SSM_CHUNK = 128
SSM_XBC = GROUP_W + 2 * SSM_GROUPS * SSM_STATE
SC_GROUPS = 4
SC_CONV = 3
D_FF = 2816
EPS = 1e-6
LOG2_E = 1.4426950408889634

GLA_COLS = 4 * GROUP_W + GLA_GATE_RANK
FOX_COLS = 3 * GROUP_W + N_HEADS
SSM_COLS = GROUP_W + SSM_XBC + N_HEADS

COL_GLA = 0
COL_FOX = COL_GLA + 4 * GROUP_W
COL_SSD = COL_FOX + 3 * GROUP_W
COL_SC = COL_SSD + GROUP_W + SSM_XBC
MAIN_COLS = COL_SC + 3 * GROUP_W

LANES = 128
SUBLANES = 8
SMALL_LR = 0
SMALL_F = 16
SMALL_DT = 24

FFN_TM = 512
FF_CHUNK = 256
PROJ_TM = 512
MIX_T = 512
ATT_TQ = 512
ATT_TK = 512
CAST_ROW_BLOCKS = 4
HALO = 8

VMEM_LIMIT = 56 * 1024 * 1024


def _const_spec(shape):
    nd = len(shape)
    return pl.BlockSpec(shape, lambda *_: (0,) * nd, pipeline_mode=pl.Buffered(1))


def _params(n_axes):
    return pltpu.CompilerParams(dimension_semantics=("arbitrary",) * n_axes,
                                vmem_limit_bytes=VMEM_LIMIT)


def _split_bf16(x, pieces):
    out, r = [], x
    for i in range(pieces):
        p = r.astype(BF16)
        out.append(p)
        if i + 1 < pieces:
            r = r - p.astype(F32)
    return out


def _dot_x01(x, m01, pieces):
    acc = None
    for p in _split_bf16(x, pieces):
        d = jnp.dot(p, m01, preferred_element_type=F32)
        acc = d if acc is None else acc + d
    return acc


def _dot_01x(m01, x, pieces):
    acc = None
    for p in _split_bf16(x, pieces):
        d = jnp.dot(m01, p, preferred_element_type=F32)
        acc = d if acc is None else acc + d
    return acc


def _dot_nt(a, b):
    return lax.dot_general(a, b, (((1,), (1,)), ((), ())), preferred_element_type=F32)


def _log_sigmoid(x):
    return jnp.minimum(x, 0.0) - jnp.log1p(jnp.exp(-jnp.abs(x)))


def _softplus(x):
    return jnp.maximum(x, 0.0) + jnp.log1p(jnp.exp(-jnp.abs(x)))


def _silu(x):
    return x * jax.nn.sigmoid(x)


def _rms_rows(x, gain):
    ms = jnp.mean(x * x, axis=-1, keepdims=True)
    return x * lax.rsqrt(ms + EPS) * gain


def _group_rms(x, ones_blocks, width, gain):
    ssq = _dot_x01(x * x, ones_blocks, 1)
    return x * lax.rsqrt(ssq * (1.0 / width) + EPS) * gain


def _head_id(shape, axis, width):
    return lax.broadcasted_iota(jnp.int32, shape, axis) // width


def _swiglu_half_step(x, g_ref, wg_ref, wu_ref, wd_ref, a_scr):
    h = _rms_rows(x, g_ref[...]).astype(BF16)
    for j in range(D_FF // FF_CHUNK):
        sl = slice(j * FF_CHUNK, (j + 1) * FF_CHUNK)
        gate = jnp.dot(h, wg_ref[:, sl], preferred_element_type=F32)
        up = jnp.dot(h, wu_ref[:, sl], preferred_element_type=F32)
        a_scr[:, sl] = (_silu(gate) * up).astype(BF16)
    return x + 0.5 * jnp.dot(a_scr[...], wd_ref[...], preferred_element_type=F32)


def _ffn_body(x_ref, g_ref, wg_ref, wu_ref, wd_ref, o_ref, a_scr):
    o_ref[...] = _swiglu_half_step(x_ref[...], g_ref, wg_ref, wu_ref, wd_ref, a_scr)


def _mix_ffn_body(x_ref, ya_ref, yb_ref, yc_ref, yd_ref, wo_ref, g_ref, wg_ref, wu_ref, wd_ref, o_ref, a_scr):
    x = x_ref[...]
    for i, y_ref in enumerate((ya_ref, yb_ref, yc_ref, yd_ref)):
        x = x + jnp.dot(y_ref[...], wo_ref[i * GROUP_W:(i + 1) * GROUP_W, :], preferred_element_type=F32)
    o_ref[...] = _swiglu_half_step(x, g_ref, wg_ref, wu_ref, wd_ref, a_scr)


def _ffn(x, gain, wg, wu, wd, mixers=None, w_out=None):
    m = x.shape[0]
    row = lambda i: (i, 0)
    ffn_specs = [_const_spec((1, D_MODEL)),
                 _const_spec((D_MODEL, D_FF)),
                 _const_spec((D_MODEL, D_FF)),
                 _const_spec((D_FF, D_MODEL))]
    if mixers is None:
        body, name, lead, lead_specs = _ffn_body, "ffn", (), []
    else:
        body, name, lead = _mix_ffn_body, "mix_ffn", tuple(mixers) + (w_out,)
        lead_specs = [pl.BlockSpec((FFN_TM, GROUP_W), row)] * len(mixers) + [_const_spec((4 * GROUP_W, D_MODEL))]
    return pl.pallas_call(
        body,
        grid=(m // FFN_TM,),
        in_specs=[pl.BlockSpec((FFN_TM, D_MODEL), row)] + lead_specs + ffn_specs,
        out_specs=pl.BlockSpec((FFN_TM, D_MODEL), row),
        out_shape=jax.ShapeDtypeStruct((m, D_MODEL), F32),
        scratch_shapes=[pltpu.VMEM((FFN_TM, D_FF), BF16)],
        compiler_params=_params(1),
        name=name,
    )(x, *lead, gain, wg, wu, wd)


def _cast_body(w_ref, o_ref):
    o_ref[...] = w_ref[...].astype(BF16)


def _to_bf16(w):
    depth, rows, cols = w.shape
    block_rows = rows // CAST_ROW_BLOCKS
    assert block_rows * CAST_ROW_BLOCKS == rows and block_rows % 16 == 0
    spec = pl.BlockSpec((1, block_rows, cols), lambda l, i: (l, i, 0))
    return pl.pallas_call(
        _cast_body,
        grid=(depth, CAST_ROW_BLOCKS),
        in_specs=[spec],
        out_specs=spec,
        out_shape=jax.ShapeDtypeStruct(w.shape, BF16),
        compiler_params=_params(2),
        name="cast_bf16",
    )(w)


def _inproj_body(x_ref, g_ref, wm_ref, ws_ref, bf_ref, qg_ref, kg_ref, e_ref, ut_ref,
                 cw_ref, cb_ref, scw_ref, scg_ref,
                 gla_ref, gvt_ref, qn_ref, kn_ref, vf_ref, f_ref, ssd_ref, sbt_ref, yd_ref, small_ref, dtr_ref,
                 xb_scr, u_scr, f_carry, *, tiles_per_seq):
    t = PROJ_TM

    @pl.when(pl.program_id(0) % tiles_per_seq == 0)
    def _():
        xb_scr[0:HALO, :] = jnp.zeros((HALO, SSM_XBC), F32)
        u_scr[0:HALO, :] = jnp.zeros((HALO, GROUP_W), F32)
        f_carry[...] = jnp.zeros_like(f_carry)

    h = _rms_rows(x_ref[...], g_ref[...]).astype(BF16)

    def proj(col):
        return jnp.dot(h, wm_ref[:, col:col + GROUP_W], preferred_element_type=F32)

    n_xbc = SSM_XBC // GROUP_W
    p_xbc = [proj(COL_SSD + GROUP_W + c * GROUP_W) for c in range(n_xbc)]
    p_sc = [proj(COL_SC + c * GROUP_W) for c in range(3)]
    p_fox = [proj(COL_FOX + c * GROUP_W) for c in range(3)]
    small = jnp.dot(h, ws_ref[...], preferred_element_type=F32)
    p_z = proj(COL_SSD)
    p_gla = [proj(COL_GLA + c * GROUP_W) for c in range(4)]

    for c in range(4):
        gla_ref[:, c * GROUP_W:(c + 1) * GROUP_W] = p_gla[c].astype(BF16)
    gvt_ref[...] = p_gla[2].T.astype(BF16)

    vf_ref[...] = p_fox[2].astype(BF16)
    q_scale = HEAD_DIM ** -0.5 * LOG2_E

    ssd_ref[:, 0:GROUP_W] = p_z.astype(BF16)
    for c in range(n_xbc):
        cols = slice(c * GROUP_W, (c + 1) * GROUP_W)
        xb_scr[HALO:HALO + t, cols] = p_xbc[c]
    for c in range(n_xbc):
        cols = slice(c * GROUP_W, (c + 1) * GROUP_W)
        conv = cb_ref[:, cols]
        for kk in range(SSM_CONV):
            conv = conv + cw_ref[kk:kk + 1, cols] * xb_scr[pl.ds(HALO - (SSM_CONV - 1) + kk, t), cols]
        act = _silu(conv)
        ssd_ref[:, GROUP_W + c * GROUP_W:2 * GROUP_W + c * GROUP_W] = act.astype(BF16)
        if c == 1:
            sbt_ref[...] = act.T.astype(BF16)
    xb_scr[0:HALO, :] = xb_scr[t:t + HALO, :]

    u_scr[HALO:HALO + t, :] = p_sc[1] * p_sc[2]
    conv = jnp.zeros((t, GROUP_W), F32)
    for kk in range(SC_CONV):
        conv = conv + scw_ref[kk:kk + 1, :] * u_scr[pl.ds(HALO - (SC_CONV - 1) + kk, t), :]
    u_scr[0:HALO, :] = u_scr[t:t + HALO, :]

    qn_ref[...] = (_group_rms(p_fox[0], e_ref[...], HEAD_DIM, qg_ref[...]) * q_scale).astype(BF16)
    kn_ref[...] = _group_rms(p_fox[1], e_ref[...], HEAD_DIM, kg_ref[...]).astype(BF16)
    yd_ref[...] = _group_rms(p_sc[0] * conv, e_ref[...], GROUP_W // SC_GROUPS, scg_ref[...]).astype(BF16)

    small_ref[...] = small
    small_t = small.T
    dtr_ref[...] = small_t[SMALL_DT:SMALL_DT + SUBLANES, :]
    log_f = _log_sigmoid(small_t[SMALL_F:SMALL_F + SUBLANES, :] + bf_ref[...])
    parts = jnp.dot(jnp.concatenate(_split_bf16(log_f, 3), axis=0), ut_ref[...], preferred_element_type=F32)
    f_ref[...] = (parts[0:SUBLANES] + parts[SUBLANES:2 * SUBLANES] + parts[2 * SUBLANES:3 * SUBLANES]
                  + f_carry[:, 0:1])
    f_carry[...] = f_carry[...] + jnp.sum(log_f, axis=-1, keepdims=True)


def _inproj(x, gain, w_main, w_small, b_forget, q_gain, k_gain, e64, utri, conv_w, conv_b,
            sc_conv_w, sc_gain, seq):
    m = x.shape[0]
    t = PROJ_TM
    row = lambda i: (i, 0)
    col = lambda i: (0, i)
    out_shape = [jax.ShapeDtypeStruct((m, 4 * GROUP_W), BF16),
                 jax.ShapeDtypeStruct((GROUP_W, m), BF16),
                 jax.ShapeDtypeStruct((m, GROUP_W), BF16),
                 jax.ShapeDtypeStruct((m, GROUP_W), BF16),
                 jax.ShapeDtypeStruct((m, GROUP_W), BF16),
                 jax.ShapeDtypeStruct((SUBLANES, m), F32),
                 jax.ShapeDtypeStruct((m, GROUP_W + SSM_XBC), BF16),
                 jax.ShapeDtypeStruct((GROUP_W, m), BF16),
                 jax.ShapeDtypeStruct((m, GROUP_W), BF16),
                 jax.ShapeDtypeStruct((m, LANES), F32),
                 jax.ShapeDtypeStruct((SUBLANES, m), F32)]
    out_specs = [pl.BlockSpec((t, 4 * GROUP_W), row),
                 pl.BlockSpec((GROUP_W, t), col),
                 pl.BlockSpec((t, GROUP_W), row),
                 pl.BlockSpec((t, GROUP_W), row),
                 pl.BlockSpec((t, GROUP_W), row),
                 pl.BlockSpec((SUBLANES, t), col),
                 pl.BlockSpec((t, GROUP_W + SSM_XBC), row),
                 pl.BlockSpec((GROUP_W, t), col),
                 pl.BlockSpec((t, GROUP_W), row),
                 pl.BlockSpec((t, LANES), row),
                 pl.BlockSpec((SUBLANES, t), col)]
    return pl.pallas_call(
        functools.partial(_inproj_body, tiles_per_seq=seq // t),
        grid=(m // t,),
        in_specs=[pl.BlockSpec((t, D_MODEL), row),
                  _const_spec((1, D_MODEL)),
                  _const_spec((D_MODEL, MAIN_COLS)),
                  _const_spec((D_MODEL, LANES)),
                  _const_spec((SUBLANES, 1)),
                  _const_spec((1, GROUP_W)),
                  _const_spec((1, GROUP_W)),
                  _const_spec((GROUP_W, GROUP_W)),
                  _const_spec((t, t)),
                  _const_spec((SSM_CONV, SSM_XBC)),
                  _const_spec((1, SSM_XBC)),
                  _const_spec((SC_CONV, GROUP_W)),
                  _const_spec((1, GROUP_W))],
        out_specs=out_specs,
        out_shape=out_shape,
        scratch_shapes=[pltpu.VMEM((t + HALO, SSM_XBC), F32),
                        pltpu.VMEM((t + HALO, GROUP_W), F32),
                        pltpu.VMEM((SUBLANES, LANES), F32)],
        compiler_params=_params(1),
        name="inproj",
    )(x, gain, w_main, w_small, b_forget, q_gain, k_gain, e64, utri, conv_w, conv_b, sc_conv_w, sc_gain)


def _gla_body(x_ref, vt_ref, sm_ref, wup_ref, bg_ref, ng_ref, lb_ref, e_ref, o_ref, st_scr, o_scr):
    t = MIX_T

    @pl.when(pl.program_id(1) == 0)
    def _():
        st_scr[...] = jnp.zeros_like(st_scr)

    q = x_ref[:, 0:GROUP_W].astype(F32) * (HEAD_DIM ** -0.5)
    k = x_ref[:, GROUP_W:2 * GROUP_W].astype(F32)
    v = x_ref[:, 2 * GROUP_W:3 * GROUP_W]
    g_out = x_ref[:, 3 * GROUP_W:4 * GROUP_W].astype(F32)

    z = jnp.dot(sm_ref[...].astype(BF16), wup_ref[...], preferred_element_type=F32) + bg_ref[...]
    log_a = _log_sigmoid(z) * (1.0 / GLA_GATE_NORMALIZER)
    c = GLA_CHUNK
    head_rows = _head_id((GROUP_W, GROUP_W), 0, HEAD_DIM)
    head_cols = _head_id((GROUP_W, GROUP_W), 1, HEAD_DIM)
    block_diag = head_rows == head_cols
    lane_head = _head_id((c, GROUP_W), 1, HEAD_DIM)
    row_id = lax.broadcasted_iota(jnp.int32, (c, GROUP_W), 0)
    key_id = lax.broadcasted_iota(jnp.int32, (c, GROUP_W), 1) % HEAD_DIM
    causal = key_id <= row_id

    n_chunks = t // c
    rows = [slice(ci * c, (ci + 1) * c) for ci in range(n_chunks)]
    b_chunks = [_dot_01x(lb_ref[...], log_a[r], 3) for r in rows]
    qd, ke, v_stack, att, decay = [], [], [], [], []
    for r, bc in zip(rows, b_chunks):
        bl = bc[c - 1:c, :]
        decay.append(jnp.exp(bl))
        qd.append((q[r] * jnp.exp(bc)).astype(BF16))
        kd = (k[r] * jnp.exp(-bc)).astype(BF16)
        ke.append((k[r] * jnp.exp(bl - bc)).astype(BF16))
        zero = jnp.zeros_like(kd)
        k_stack = jnp.concatenate([jnp.where(lane_head == h, kd, zero) for h in range(N_HEADS)], axis=0)
        v_stack.append(jnp.concatenate([jnp.where(lane_head == h, v[r], zero) for h in range(N_HEADS)], axis=0))
        att.append(_dot_nt(qd[-1], k_stack))

    kv_t = []
    for ci in range(n_chunks):
        half = jnp.zeros_like(ke[ci])
        ke_pair = jnp.concatenate([ke[ci], half] if ci % 2 == 0 else [half, ke[ci]], axis=0)
        kv_t.append(jnp.dot(vt_ref[:, (ci // 2) * LANES:(ci // 2 + 1) * LANES], ke_pair,
                            preferred_element_type=F32))

    st = st_scr[...]
    for ci in range(n_chunks):
        att_c = jnp.where(causal, att[ci], 0.0).astype(BF16)
        o_scr[rows[ci], :] = (jnp.dot(att_c, v_stack[ci], preferred_element_type=F32)
                              + _dot_nt(qd[ci], st.astype(BF16)))
        st = st * decay[ci] + jnp.where(block_diag, kv_t[ci], 0.0)
    st_scr[...] = st

    y = _group_rms(o_scr[...], e_ref[...], HEAD_DIM, ng_ref[...]) * _silu(g_out)
    o_ref[...] = y.astype(BF16)


def _gla(gla_in, gla_vt, small, wup_pad, b_gate, norm_g, lb, e64, batch, seq):
    nt = seq // MIX_T
    row = lambda b, t: (b * nt + t, 0)
    return pl.pallas_call(
        _gla_body,
        grid=(batch, nt),
        in_specs=[pl.BlockSpec((MIX_T, 4 * GROUP_W), row),
                  pl.BlockSpec((GROUP_W, MIX_T), lambda b, t: (0, b * nt + t)),
                  pl.BlockSpec((MIX_T, LANES), row),
                  _const_spec((LANES, GROUP_W)),
                  _const_spec((1, GROUP_W)),
                  _const_spec((1, GROUP_W)),
                  _const_spec((GLA_CHUNK, GLA_CHUNK)),
                  _const_spec((GROUP_W, GROUP_W))],
        out_specs=pl.BlockSpec((MIX_T, GROUP_W), row),
        out_shape=jax.ShapeDtypeStruct((batch * seq, GROUP_W), BF16),
        scratch_shapes=[pltpu.VMEM((GROUP_W, GROUP_W), F32),
                        pltpu.VMEM((MIX_T, GROUP_W), F32)],
        compiler_params=_params(2),
        name="gla",
    )(gla_in, gla_vt, small, wup_pad, b_gate, norm_g, lb, e64)


def _attn_body(q_ref, k_ref, v_ref, f_ref, og_ref, o_ref, qs_scr, m_scr, acc_scr, p0_scr, al0_scr, p1_scr, al1_scr):
    tq, tk = ATT_TQ, ATT_TK
    i = pl.program_id(1)
    q0 = pl.multiple_of(i * tq, tq)

    q = q_ref[...]
    lane_head = _head_id((tq, GROUP_W), 1, HEAD_DIM)
    for h in range(N_HEADS):
        qs_scr[h * tq:(h + 1) * tq, :] = jnp.where(lane_head == h, q, jnp.zeros_like(q))
    m_scr[...] = jnp.full_like(m_scr, -jnp.inf)
    acc_scr[...] = jnp.zeros_like(acc_scr)

    f_here = f_ref[:, pl.ds(q0, LANES)][:, 0:1]
    key_lane_head = _head_id((tk, GROUP_W), 1, HEAD_DIM)

    def scores_to_probs(j, masked, p_scr, al_scr):
        c0 = pl.multiple_of(j * tk, tk)
        k = k_ref[pl.ds(c0, tk), :]
        bias = (f_here - f_ref[:, pl.ds(c0, tk)]) * LOG2_E
        if masked:
            keep = (lax.broadcasted_iota(jnp.int32, (tq, tk), 1) + c0
                    <= lax.broadcasted_iota(jnp.int32, (tq, tk), 0) + q0)
        for h in range(N_HEADS):
            hs = slice(h * tq, (h + 1) * tq)
            s = _dot_nt(qs_scr[hs, :], k) + bias[h:h + 1, :]
            if masked:
                s = jnp.where(keep, s, -jnp.inf)
            m_prev = m_scr[hs]
            m_new = jnp.maximum(m_prev, jnp.max(s, axis=-1, keepdims=True))
            al_scr[hs] = jnp.exp2(m_prev - m_new)
            p_scr[hs, :] = jnp.exp2(s - jnp.tile(m_new, (1, tk // LANES))).astype(BF16)
            m_scr[hs] = m_new

    def accumulate(j, p_scr, al_scr):
        c0 = pl.multiple_of(j * tk, tk)
        v = v_ref[pl.ds(c0, tk), :]
        for h in range(N_HEADS):
            hs = slice(h * tq, (h + 1) * tq)
            v_h = jnp.where(key_lane_head == (h + 1) % N_HEADS, jnp.ones_like(v), v)
            acc_scr[hs, :] = (acc_scr[hs, :] * jnp.tile(al_scr[hs], (1, GROUP_W // LANES))
                              + jnp.dot(p_scr[hs, :], v_h, preferred_element_type=F32))

    n_full = (i * tq) // tk
    n_loop = jnp.maximum(n_full - 1, 0)
    buf0, buf1 = (p0_scr, al0_scr), (p1_scr, al1_scr)

    scores_to_probs(0, True, *buf1)

    def pair_body(pair, carry):
        j = 1 + 2 * pair
        accumulate(j - 1, *buf1)
        scores_to_probs(j, False, *buf0)
        accumulate(j, *buf0)
        scores_to_probs(j + 1, False, *buf1)
        return carry

    lax.fori_loop(0, n_loop // 2, pair_body, 0)

    @pl.when(n_loop % 2 == 1)
    def _():
        accumulate(n_full - 2, *buf1)
        scores_to_probs(n_full - 1, False, *buf1)

    @pl.when(n_full >= 1)
    def _():
        accumulate(n_full - 1, *buf1)
        scores_to_probs(n_full, True, *buf1)

    accumulate(n_full, *buf1)

    y = jnp.zeros((tq, GROUP_W), F32)
    for h in range(N_HEADS):
        acc = acc_scr[h * tq:(h + 1) * tq, :]
        row_sum = jnp.max(jnp.where(lane_head == (h + 1) % N_HEADS, acc, -jnp.inf), axis=-1, keepdims=True)
        oh = acc / row_sum
        mine = lane_head == h
        ssq = jnp.sum(jnp.where(mine, oh * oh, 0.0), axis=-1, keepdims=True)
        y = jnp.where(mine, oh * lax.rsqrt(ssq * (1.0 / HEAD_DIM) + EPS), y)
    o_ref[...] = (y * og_ref[...]).astype(BF16)


def _attn(qn, kn, vf, f_cum, out_gain, batch, seq):
    assert ATT_TQ == ATT_TK, "the key-tile pipeline assumes exactly one tile on the causal diagonal"
    nq = seq // ATT_TQ
    rows = N_HEADS * ATT_TQ
    return pl.pallas_call(
        _attn_body,
        grid=(batch, nq),
        in_specs=[pl.BlockSpec((ATT_TQ, GROUP_W), lambda b, i: (b * nq + i, 0)),
                  pl.BlockSpec((seq, GROUP_W), lambda b, i: (b, 0)),
                  pl.BlockSpec((seq, GROUP_W), lambda b, i: (b, 0)),
                  pl.BlockSpec((SUBLANES, seq), lambda b, i: (0, b)),
                  _const_spec((1, GROUP_W))],
        out_specs=pl.BlockSpec((ATT_TQ, GROUP_W), lambda b, i: (b * nq + i, 0)),
        out_shape=jax.ShapeDtypeStruct((batch * seq, GROUP_W), BF16),
        scratch_shapes=[pltpu.VMEM((rows, GROUP_W), BF16),
                        pltpu.VMEM((rows, LANES), F32),
                        pltpu.VMEM((rows, GROUP_W), F32),
                        pltpu.VMEM((rows, ATT_TK), BF16),
                        pltpu.VMEM((rows, LANES), F32),
                        pltpu.VMEM((rows, ATT_TK), BF16),
                        pltpu.VMEM((rows, LANES), F32)],
        compiler_params=_params(2),
        name="fox_attn",
    )(qn, kn, vf, f_cum, out_gain)


def _ssd_body(x_ref, bt_ref, sm_ref, dtr_ref, dtb_ref, dtbr_ref, anr_ref, seldt_ref, dskip_ref, ng_ref, ub_ref,
              o_ref, st_scr):
    t, qc, n = MIX_T, SSM_CHUNK, SSM_STATE

    @pl.when(pl.program_id(1) == 0)
    def _():
        st_scr[...] = jnp.zeros_like(st_scr)

    gate = _silu(x_ref[:, 0:GROUP_W].astype(F32))
    xs = x_ref[:, GROUP_W:2 * GROUP_W].astype(F32)
    dt_col = _softplus(sm_ref[...] + dtb_ref[...])
    xd = (xs * _dot_x01(dt_col, seldt_ref[...], 2)).astype(BF16)

    a_row = _softplus(dtr_ref[...] + dtbr_ref[...]) * anr_ref[...]
    pieces = jnp.dot(jnp.concatenate(_split_bf16(a_row, 3), axis=0), ub_ref[...], preferred_element_type=F32)
    acs_row = pieces[0:SUBLANES] + pieces[SUBLANES:2 * SUBLANES] + pieces[2 * SUBLANES:3 * SUBLANES]

    lane_head = _head_id((qc, GROUP_W), 1, HEAD_DIM)
    tril = (lax.broadcasted_iota(jnp.int32, (qc, qc), 1)
            <= lax.broadcasted_iota(jnp.int32, (qc, qc), 0))
    heads_per_group = N_HEADS // SSM_GROUPS
    state_block = (_head_id((N_HEADS * n, GROUP_W), 0, n) == _head_id((N_HEADS * n, GROUP_W), 1, HEAD_DIM))

    n_chunks = t // qc
    rows = [slice(ci * qc, (ci + 1) * qc) for ci in range(n_chunks)]
    bt_grp = [[bt_ref[g * n:(g + 1) * n, r] for g in range(SSM_GROUPS)] for r in rows]
    c_grp = [[x_ref[r, 2 * GROUP_W + (SSM_GROUPS + g) * n:2 * GROUP_W + (SSM_GROUPS + g + 1) * n]
              for g in range(SSM_GROUPS)] for r in rows]
    cb = [[jnp.dot(c_grp[ci][g], bt_grp[ci][g], preferred_element_type=F32) for g in range(SSM_GROUPS)]
          for ci in range(n_chunks)]

    lhs_parts, decays, incs = [], [], []
    for ci, r in enumerate(rows):
        scores, c_scaled, b_decayed, e_last = [], [], [], []
        for h in range(N_HEADS):
            g = h // heads_per_group
            ac_t = jnp.broadcast_to(acs_row[h:h + 1, r], (qc, qc))
            ac = ac_t.T
            lmat = jnp.exp(jnp.where(tril, ac - ac_t, -jnp.inf))
            scores.append((cb[ci][g] * lmat).astype(BF16))
            a_last = ac[qc - 1:qc, :]
            e_last.append(jnp.exp(a_last))
            b_decayed.append((bt_grp[ci][g].astype(F32) * jnp.exp(a_last - ac_t[0:1, :])).astype(BF16))
            c_scaled.append((c_grp[ci][g].astype(F32) * jnp.exp(ac)).astype(BF16))
        lhs_parts.append(jnp.concatenate(scores + c_scaled, axis=1))
        decays.append(jnp.concatenate([jnp.broadcast_to(jnp.tile(e, (1, GROUP_W // LANES)), (n, GROUP_W))
                                       for e in e_last], axis=0))
        incs.append(jnp.dot(jnp.concatenate(b_decayed, axis=0), xd[r], preferred_element_type=F32))

    st = st_scr[...]
    for ci, r in enumerate(rows):
        xd_c = xd[r]
        zero = jnp.zeros_like(xd_c)
        xd_heads = [jnp.where(lane_head == h, xd_c, zero) for h in range(N_HEADS)]
        rhs = jnp.concatenate(xd_heads + [st.astype(BF16)], axis=0)
        y = jnp.dot(lhs_parts[ci], rhs, preferred_element_type=F32)
        st = st * decays[ci] + jnp.where(state_block, incs[ci], 0.0)

        y = (y + xs[r] * dskip_ref[...]) * gate[r]
        for g in range(SSM_GROUPS):
            gs = slice(g * LANES, (g + 1) * LANES)
            yg = y[:, gs]
            ms = jnp.mean(yg * yg, axis=-1, keepdims=True)
            o_ref[r, gs] = (yg * lax.rsqrt(ms + EPS) * ng_ref[:, gs]).astype(BF16)
    st_scr[...] = st


def _ssd(ssd_in, ssd_bt, small, dt_rows, dtb, dtb_col, a_neg_col, seldt, d_skip, norm_g, utri_chunks, batch, seq):
    nt = seq // MIX_T
    row = lambda b, t: (b * nt + t, 0)
    return pl.pallas_call(
        _ssd_body,
        grid=(batch, nt),
        in_specs=[pl.BlockSpec((MIX_T, GROUP_W + SSM_XBC), row),
                  pl.BlockSpec((GROUP_W, MIX_T), lambda b, t: (0, b * nt + t)),
                  pl.BlockSpec((MIX_T, LANES), row),
                  pl.BlockSpec((SUBLANES, MIX_T), lambda b, t: (0, b * nt + t)),
                  _const_spec((1, LANES)),
                  _const_spec((SUBLANES, 1)),
                  _const_spec((SUBLANES, 1)),
                  _const_spec((LANES, GROUP_W)),
                  _const_spec((1, GROUP_W)),
                  _const_spec((1, GROUP_W)),
                  _const_spec((MIX_T, MIX_T))],
        out_specs=pl.BlockSpec((MIX_T, GROUP_W), row),
        out_shape=jax.ShapeDtypeStruct((batch * seq, GROUP_W), BF16),
        scratch_shapes=[pltpu.VMEM((N_HEADS * SSM_STATE, GROUP_W), F32)],
        compiler_params=_params(2),
        name="ssd",
    )(ssd_in, ssd_bt, small, dt_rows, dtb, dtb_col, a_neg_col, seldt, d_skip, norm_g, utri_chunks)


def _block_diag_ones(size, block):
    idx = np.arange(size) // block
    return jnp.asarray(idx[:, None] == idx[None, :], BF16)


def _chunk_lower_tri(size, block):
    i = np.arange(size)
    same = (i[:, None] // block) == (i[None, :] // block)
    return jnp.asarray(same & (i[None, :] <= i[:, None]), BF16)


def _lane_spread(src_lane0, width):
    m = np.zeros((LANES, N_HEADS * width))
    for h in range(N_HEADS):
        m[src_lane0 + h, h * width:(h + 1) * width] = 1.0
    return jnp.asarray(m, BF16)


def _row(v):
    return v.reshape(1, -1).astype(F32)


def _lane_slot(values, offset, width):
    return jnp.zeros((1, width), F32).at[0, offset:offset + values.shape[0]].set(values.astype(F32))


def _sublane_col(values):
    return jnp.zeros((SUBLANES, 1), F32).at[0:values.shape[0], 0].set(values.astype(F32))


def _regroup_w_in(w_in):
    gla_end = 4 * GROUP_W
    fox0 = GLA_COLS
    fox_end = fox0 + 3 * GROUP_W
    ssm0 = GLA_COLS + FOX_COLS
    ssm_end = ssm0 + GROUP_W + SSM_XBC
    sc0 = GLA_COLS + FOX_COLS + SSM_COLS
    w_main = jnp.concatenate([w_in[:, 0:gla_end], w_in[:, fox0:fox_end],
                              w_in[:, ssm0:ssm_end], w_in[:, sc0:]], axis=1)
    w_small = jnp.zeros((D_MODEL, LANES), w_in.dtype)
    w_small = w_small.at[:, SMALL_LR:SMALL_LR + GLA_GATE_RANK].set(w_in[:, gla_end:GLA_COLS])
    w_small = w_small.at[:, SMALL_F:SMALL_F + N_HEADS].set(w_in[:, fox_end:ssm0])
    w_small = w_small.at[:, SMALL_DT:SMALL_DT + N_HEADS].set(w_in[:, ssm_end:sc0])
    return w_main, w_small


def kernel(x, ffn1_norm, ffn1_w_gate, ffn1_w_up, ffn1_w_down, mix_norm, w_in, gla_w_gate_up, gla_b_gate, gla_norm, fox_b_forget, fox_q_norm, fox_k_norm, fox_out_norm, ssm_conv_w, ssm_conv_b, ssm_dt_bias, ssm_A_log, ssm_D, ssm_norm, sc_conv_w, sc_out_norm, w_out, ffn2_norm, ffn2_w_gate, ffn2_w_up, ffn2_w_down):
    batch, seq, _ = x.shape
    depth = w_in.shape[0]
    assert seq % MIX_T == 0 and seq % PROJ_TM == 0 and seq % ATT_TK == 0 and seq % ATT_TQ == 0
    assert (batch * seq) % FFN_TM == 0

    e64 = _block_diag_ones(GROUP_W, HEAD_DIM)
    lb_gla = _chunk_lower_tri(GLA_CHUNK, GLA_CHUNK)
    utri_t = jnp.asarray(np.triu(np.ones((PROJ_TM, PROJ_TM))), BF16)
    ub_ssd = _chunk_lower_tri(MIX_T, SSM_CHUNK).T
    seldt = _lane_spread(SMALL_DT, HEAD_DIM)

    w1g, w1u, w1d = _to_bf16(ffn1_w_gate), _to_bf16(ffn1_w_up), _to_bf16(ffn1_w_down)
    w2g, w2u, w2d = _to_bf16(ffn2_w_gate), _to_bf16(ffn2_w_up), _to_bf16(ffn2_w_down)
    wo, w_in_b = _to_bf16(w_out), _to_bf16(w_in)

    xf = x.reshape(batch * seq, D_MODEL)
    for l in range(depth):
        xf = _ffn(xf, _row(ffn1_norm[l]), w1g[l], w1u[l], w1d[l])

        w_main, w_small = _regroup_w_in(w_in_b[l])
        gla_in, gla_vt, qn, kn, vf, f_cum, ssd_in, ssd_bt, yd, small, dt_rows = _inproj(
            xf, _row(mix_norm[l]), w_main, w_small, _sublane_col(fox_b_forget[l]),
            _row(jnp.tile(fox_q_norm[l], N_HEADS)), _row(jnp.tile(fox_k_norm[l], N_HEADS)),
            e64, utri_t, ssm_conv_w[l].astype(F32), _row(ssm_conv_b[l]),
            sc_conv_w[l].astype(F32), _row(sc_out_norm[l]), seq)

        wup_pad = jnp.zeros((LANES, GROUP_W), F32).at[SMALL_LR:SMALL_LR + GLA_GATE_RANK].set(
            gla_w_gate_up[l]).astype(BF16)
        ya = _gla(gla_in, gla_vt, small, wup_pad, _row(gla_b_gate[l]), _row(gla_norm[l]), lb_gla, e64, batch, seq)

        yb = _attn(qn, kn, vf, f_cum, _row(fox_out_norm[l]), batch, seq)

        a_neg = -jnp.exp(ssm_A_log[l].astype(F32))
        yc = _ssd(ssd_in, ssd_bt, small, dt_rows, _lane_slot(ssm_dt_bias[l], SMALL_DT, LANES),
                  _sublane_col(ssm_dt_bias[l]), _sublane_col(a_neg), seldt,
                  _row(jnp.repeat(ssm_D[l], HEAD_DIM)), _row(ssm_norm[l]), ub_ssd, batch, seq)

        xf = _ffn(xf, _row(ffn2_norm[l]), w2g[l], w2u[l], w2d[l], mixers=(ya, yb, yc, yd), w_out=wo[l])
    return xf.reshape(batch, seq, D_MODEL)
```

```python
import functools

import numpy as np
import jax
import jax.numpy as jnp
from jax import lax
from jax.experimental import pallas as pl
from jax.experimental.pallas import tpu as pltpu

F32, BF16 = jnp.float32, jnp.bfloat16

D_MODEL = 1024
GROUP_W = 256
HEAD_DIM = 64
N_HEADS = GROUP_W // HEAD_DIM
GLA_GATE_RANK = 16
GLA_GATE_NORMALIZER = 16.0
GLA_CHUNK = 64
SSM_GROUPS = 2
SSM_STATE = 128
SSM_CONV = 4
SSM_CHUNK = 128
SSM_XBC = GROUP_W + 2 * SSM_GROUPS * SSM_STATE
SC_GROUPS = 4
SC_CONV = 3
D_FF = 2816
EPS = 1e-6
LOG2_E = 1.4426950408889634

GLA_COLS = 4 * GROUP_W + GLA_GATE_RANK
FOX_COLS = 3 * GROUP_W + N_HEADS
SSM_COLS = GROUP_W + SSM_XBC + N_HEADS

COL_GLA = 0
COL_FOX = COL_GLA + 4 * GROUP_W
COL_SSD = COL_FOX + 3 * GROUP_W
COL_SC = COL_SSD + GROUP_W + SSM_XBC
MAIN_COLS = COL_SC + 3 * GROUP_W

LANES = 128
SUBLANES = 8
SMALL_LR = 0
SMALL_F = 16
SMALL_DT = 24

FFN_TM = 1024
FF_CHUNK = 256
PROJ_TM = 1024
MIX_T = 1024
ATT_TQ = 512
ATT_TK = 512
CAST_ROW_BLOCKS = 4
HALO = 8

VMEM_LIMIT = 56 * 1024 * 1024


def _const_spec(shape):
    nd = len(shape)
    return pl.BlockSpec(shape, lambda *_: (0,) * nd, pipeline_mode=pl.Buffered(1))


def _params(n_axes):
    return pltpu.CompilerParams(dimension_semantics=("arbitrary",) * n_axes,
                                vmem_limit_bytes=VMEM_LIMIT)


def _split_bf16(x, pieces):
    out, r = [], x
    for i in range(pieces):
        p = r.astype(BF16)
        out.append(p)
        if i + 1 < pieces:
            r = r - p.astype(F32)
    return out


def _dot_x01(x, m01, pieces):
    acc = None
    for p in _split_bf16(x, pieces):
        d = jnp.dot(p, m01, preferred_element_type=F32)
        acc = d if acc is None else acc + d
    return acc


def _dot_01x(m01, x, pieces):
    acc = None
    for p in _split_bf16(x, pieces):
        d = jnp.dot(m01, p, preferred_element_type=F32)
        acc = d if acc is None else acc + d
    return acc


def _dot_nt(a, b):
    return lax.dot_general(a, b, (((1,), (1,)), ((), ())), preferred_element_type=F32)


def _log_sigmoid(x):
    return jnp.minimum(x, 0.0) - jnp.log1p(jnp.exp(-jnp.abs(x)))


def _softplus(x):
    return jnp.maximum(x, 0.0) + jnp.log1p(jnp.exp(-jnp.abs(x)))


def _silu(x):
    return x * jax.nn.sigmoid(x)


def _rms_rows(x, gain):
    ms = jnp.mean(x * x, axis=-1, keepdims=True)
    return x * lax.rsqrt(ms + EPS) * gain


def _group_rms(x, ones_blocks, width, gain):
    ssq = _dot_x01(x * x, ones_blocks, 1)
    return x * lax.rsqrt(ssq * (1.0 / width) + EPS) * gain


def _head_id(shape, axis, width):
    return lax.broadcasted_iota(jnp.int32, shape, axis) // width


def _swiglu_half_step(x, g_ref, wg_ref, wu_ref, wd_ref, a_scr):
    h = _rms_rows(x, g_ref[...]).astype(BF16)
    for j in range(D_FF // FF_CHUNK):
        sl = slice(j * FF_CHUNK, (j + 1) * FF_CHUNK)
        gate = jnp.dot(h, wg_ref[:, sl], preferred_element_type=F32)
        up = jnp.dot(h, wu_ref[:, sl], preferred_element_type=F32)
        a_scr[:, sl] = (_silu(gate) * up).astype(BF16)
    return x + 0.5 * jnp.dot(a_scr[...], wd_ref[...], preferred_element_type=F32)


def _ffn_body(x_ref, g_ref, wg_ref, wu_ref, wd_ref, o_ref, a_scr):
    o_ref[...] = _swiglu_half_step(x_ref[...], g_ref, wg_ref, wu_ref, wd_ref, a_scr)


def _mix_ffn_body(x_ref, ya_ref, yb_ref, yc_ref, yd_ref, wo_ref, g_ref, wg_ref, wu_ref, wd_ref, o_ref, a_scr):
    x = x_ref[...]
    for i, y_ref in enumerate((ya_ref, yb_ref, yc_ref, yd_ref)):
        x = x + jnp.dot(y_ref[...], wo_ref[i * GROUP_W:(i + 1) * GROUP_W, :], preferred_element_type=F32)
    o_ref[...] = _swiglu_half_step(x, g_ref, wg_ref, wu_ref, wd_ref, a_scr)


def _ffn(x, gain, wg, wu, wd, mixers=None, w_out=None):
    m = x.shape[0]
    row = lambda i: (i, 0)
    ffn_specs = [_const_spec((1, D_MODEL)),
                 _const_spec((D_MODEL, D_FF)),
                 _const_spec((D_MODEL, D_FF)),
                 _const_spec((D_FF, D_MODEL))]
    if mixers is None:
        body, name, lead, lead_specs = _ffn_body, "ffn", (), []
    else:
        body, name, lead = _mix_ffn_body, "mix_ffn", tuple(mixers) + (w_out,)
        lead_specs = [pl.BlockSpec((FFN_TM, GROUP_W), row)] * len(mixers) + [_const_spec((4 * GROUP_W, D_MODEL))]
    return pl.pallas_call(
        body,
        grid=(m // FFN_TM,),
        in_specs=[pl.BlockSpec((FFN_TM, D_MODEL), row)] + lead_specs + ffn_specs,
        out_specs=pl.BlockSpec((FFN_TM, D_MODEL), row),
        out_shape=jax.ShapeDtypeStruct((m, D_MODEL), F32),
        scratch_shapes=[pltpu.VMEM((FFN_TM, D_FF), BF16)],
        compiler_params=_params(1),
        name=name,
    )(x, *lead, gain, wg, wu, wd)


def _cast_body(w_ref, o_ref):
    o_ref[...] = w_ref[...].astype(BF16)


def _to_bf16(w):
    depth, rows, cols = w.shape
    block_rows = rows // CAST_ROW_BLOCKS
    assert block_rows * CAST_ROW_BLOCKS == rows and block_rows % 16 == 0
    spec = pl.BlockSpec((1, block_rows, cols), lambda l, i: (l, i, 0))
    return pl.pallas_call(
        _cast_body,
        grid=(depth, CAST_ROW_BLOCKS),
        in_specs=[spec],
        out_specs=spec,
        out_shape=jax.ShapeDtypeStruct(w.shape, BF16),
        compiler_params=_params(2),
        name="cast_bf16",
    )(w)


def _inproj_body(x_ref, g_ref, wm_ref, ws_ref, bf_ref, qg_ref, kg_ref, e_ref, ut_ref,
                 cw_ref, cb_ref, scw_ref, scg_ref,
                 gla_ref, gvt_ref, qn_ref, kn_ref, vf_ref, f_ref, ssd_ref, sbt_ref, yd_ref, small_ref, dtr_ref,
                 xb_scr, u_scr, f_carry, *, tiles_per_seq):
    t = PROJ_TM

    @pl.when(pl.program_id(0) % tiles_per_seq == 0)
    def _():
        xb_scr[0:HALO, :] = jnp.zeros((HALO, SSM_XBC), F32)
        u_scr[0:HALO, :] = jnp.zeros((HALO, GROUP_W), F32)
        f_carry[...] = jnp.zeros_like(f_carry)

    h = _rms_rows(x_ref[...], g_ref[...]).astype(BF16)

    def proj(col):
        return jnp.dot(h, wm_ref[:, col:col + GROUP_W], preferred_element_type=F32)

    n_xbc = SSM_XBC // GROUP_W
    p_xbc = [proj(COL_SSD + GROUP_W + c * GROUP_W) for c in range(n_xbc)]
    p_sc = [proj(COL_SC + c * GROUP_W) for c in range(3)]
    p_fox = [proj(COL_FOX + c * GROUP_W) for c in range(3)]
    small = jnp.dot(h, ws_ref[...], preferred_element_type=F32)
    p_z = proj(COL_SSD)
    p_gla = [proj(COL_GLA + c * GROUP_W) for c in range(4)]

    for c in range(4):
        gla_ref[:, c * GROUP_W:(c + 1) * GROUP_W] = p_gla[c].astype(BF16)
    gvt_ref[...] = p_gla[2].T.astype(BF16)

    vf_ref[...] = p_fox[2].astype(BF16)
    q_scale = HEAD_DIM ** -0.5 * LOG2_E

    ssd_ref[:, 0:GROUP_W] = p_z.astype(BF16)
    for c in range(n_xbc):
        cols = slice(c * GROUP_W, (c + 1) * GROUP_W)
        xb_scr[HALO:HALO + t, cols] = p_xbc[c]
    for c in range(n_xbc):
        cols = slice(c * GROUP_W, (c + 1) * GROUP_W)
        conv = cb_ref[:, cols]
        for kk in range(SSM_CONV):
            conv = conv + cw_ref[kk:kk + 1, cols] * xb_scr[pl.ds(HALO - (SSM_CONV - 1) + kk, t), cols]
        act = _silu(conv)
        ssd_ref[:, GROUP_W + c * GROUP_W:2 * GROUP_W + c * GROUP_W] = act.astype(BF16)
        if c == 1:
            sbt_ref[...] = act.T.astype(BF16)
    xb_scr[0:HALO, :] = xb_scr[t:t + HALO, :]

    u_scr[HALO:HALO + t, :] = p_sc[1] * p_sc[2]
    conv = jnp.zeros((t, GROUP_W), F32)
    for kk in range(SC_CONV):
        conv = conv + scw_ref[kk:kk + 1, :] * u_scr[pl.ds(HALO - (SC_CONV - 1) + kk, t), :]
    u_scr[0:HALO, :] = u_scr[t:t + HALO, :]

    qn_ref[...] = (_group_rms(p_fox[0], e_ref[...], HEAD_DIM, qg_ref[...]) * q_scale).astype(BF16)
    kn_ref[...] = _group_rms(p_fox[1], e_ref[...], HEAD_DIM, kg_ref[...]).astype(BF16)
    yd_ref[...] = _group_rms(p_sc[0] * conv, e_ref[...], GROUP_W // SC_GROUPS, scg_ref[...]).astype(BF16)

    small_ref[...] = small
    small_t = small.T
    dtr_ref[...] = small_t[SMALL_DT:SMALL_DT + SUBLANES, :]
    log_f = _log_sigmoid(small_t[SMALL_F:SMALL_F + SUBLANES, :] + bf_ref[...])
    parts = jnp.dot(jnp.concatenate(_split_bf16(log_f, 3), axis=0), ut_ref[...], preferred_element_type=F32)
    f_ref[...] = (parts[0:SUBLANES] + parts[SUBLANES:2 * SUBLANES] + parts[2 * SUBLANES:3 * SUBLANES]
                  + f_carry[:, 0:1])
    f_carry[...] = f_carry[...] + jnp.sum(log_f, axis=-1, keepdims=True)


def _inproj(x, gain, w_main, w_small, b_forget, q_gain, k_gain, e64, utri, conv_w, conv_b,
            sc_conv_w, sc_gain, seq):
    m = x.shape[0]
    t = PROJ_TM
    row = lambda i: (i, 0)
    col = lambda i: (0, i)
    out_shape = [jax.ShapeDtypeStruct((m, 4 * GROUP_W), BF16),
                 jax.ShapeDtypeStruct((GROUP_W, m), BF16),
                 jax.ShapeDtypeStruct((m, GROUP_W), BF16),
                 jax.ShapeDtypeStruct((m, GROUP_W), BF16),
                 jax.ShapeDtypeStruct((m, GROUP_W), BF16),
                 jax.ShapeDtypeStruct((SUBLANES, m), F32),
                 jax.ShapeDtypeStruct((m, GROUP_W + SSM_XBC), BF16),
                 jax.ShapeDtypeStruct((GROUP_W, m), BF16),
                 jax.ShapeDtypeStruct((m, GROUP_W), BF16),
                 jax.ShapeDtypeStruct((m, LANES), F32),
                 jax.ShapeDtypeStruct((SUBLANES, m), F32)]
    out_specs = [pl.BlockSpec((t, 4 * GROUP_W), row),
                 pl.BlockSpec((GROUP_W, t), col),
                 pl.BlockSpec((t, GROUP_W), row),
                 pl.BlockSpec((t, GROUP_W), row),
                 pl.BlockSpec((t, GROUP_W), row),
                 pl.BlockSpec((SUBLANES, t), col),
                 pl.BlockSpec((t, GROUP_W + SSM_XBC), row),
                 pl.BlockSpec((GROUP_W, t), col),
                 pl.BlockSpec((t, GROUP_W), row),
                 pl.BlockSpec((t, LANES), row),
                 pl.BlockSpec((SUBLANES, t), col)]
    return pl.pallas_call(
        functools.partial(_inproj_body, tiles_per_seq=seq // t),
        grid=(m // t,),
        in_specs=[pl.BlockSpec((t, D_MODEL), row),
                  _const_spec((1, D_MODEL)),
                  _const_spec((D_MODEL, MAIN_COLS)),
                  _const_spec((D_MODEL, LANES)),
                  _const_spec((SUBLANES, 1)),
                  _const_spec((1, GROUP_W)),
                  _const_spec((1, GROUP_W)),
                  _const_spec((GROUP_W, GROUP_W)),
                  _const_spec((t, t)),
                  _const_spec((SSM_CONV, SSM_XBC)),
                  _const_spec((1, SSM_XBC)),
                  _const_spec((SC_CONV, GROUP_W)),
                  _const_spec((1, GROUP_W))],
        out_specs=out_specs,
        out_shape=out_shape,
        scratch_shapes=[pltpu.VMEM((t + HALO, SSM_XBC), F32),
                        pltpu.VMEM((t + HALO, GROUP_W), F32),
                        pltpu.VMEM((SUBLANES, LANES), F32)],
        compiler_params=_params(1),
        name="inproj",
    )(x, gain, w_main, w_small, b_forget, q_gain, k_gain, e64, utri, conv_w, conv_b, sc_conv_w, sc_gain)


def _gla_body(x_ref, vt_ref, sm_ref, wup_ref, bg_ref, ng_ref, lb_ref, e_ref, o_ref, st_scr, o_scr):
    t = MIX_T

    @pl.when(pl.program_id(1) == 0)
    def _():
        st_scr[...] = jnp.zeros_like(st_scr)

    q = x_ref[:, 0:GROUP_W].astype(F32) * (HEAD_DIM ** -0.5)
    k = x_ref[:, GROUP_W:2 * GROUP_W].astype(F32)
    v = x_ref[:, 2 * GROUP_W:3 * GROUP_W]
    g_out = x_ref[:, 3 * GROUP_W:4 * GROUP_W].astype(F32)

    z = jnp.dot(sm_ref[...].astype(BF16), wup_ref[...], preferred_element_type=F32) + bg_ref[...]
    log_a = _log_sigmoid(z) * (1.0 / GLA_GATE_NORMALIZER)
    c = GLA_CHUNK
    head_rows = _head_id((GROUP_W, GROUP_W), 0, HEAD_DIM)
    head_cols = _head_id((GROUP_W, GROUP_W), 1, HEAD_DIM)
    block_diag = head_rows == head_cols
    lane_head = _head_id((c, GROUP_W), 1, HEAD_DIM)
    row_id = lax.broadcasted_iota(jnp.int32, (c, GROUP_W), 0)
    key_id = lax.broadcasted_iota(jnp.int32, (c, GROUP_W), 1) % HEAD_DIM
    causal = key_id <= row_id

    n_chunks = t // c
    rows = [slice(ci * c, (ci + 1) * c) for ci in range(n_chunks)]
    b_chunks = [_dot_01x(lb_ref[...], log_a[r], 3) for r in rows]
    qd, ke, v_stack, att, decay = [], [], [], [], []
    for r, bc in zip(rows, b_chunks):
        bl = bc[c - 1:c, :]
        decay.append(jnp.exp(bl))
        qd.append((q[r] * jnp.exp(bc)).astype(BF16))
        kd = (k[r] * jnp.exp(-bc)).astype(BF16)
        ke.append((k[r] * jnp.exp(bl - bc)).astype(BF16))
        zero = jnp.zeros_like(kd)
        k_stack = jnp.concatenate([jnp.where(lane_head == h, kd, zero) for h in range(N_HEADS)], axis=0)
        v_stack.append(jnp.concatenate([jnp.where(lane_head == h, v[r], zero) for h in range(N_HEADS)], axis=0))
        att.append(_dot_nt(qd[-1], k_stack))

    kv_t = []
    for ci in range(n_chunks):
        half = jnp.zeros_like(ke[ci])
        ke_pair = jnp.concatenate([ke[ci], half] if ci % 2 == 0 else [half, ke[ci]], axis=0)
        kv_t.append(jnp.dot(vt_ref[:, (ci // 2) * LANES:(ci // 2 + 1) * LANES], ke_pair,
                            preferred_element_type=F32))

    st = st_scr[...]
    for ci in range(n_chunks):
        att_c = jnp.where(causal, att[ci], 0.0).astype(BF16)
        o_scr[rows[ci], :] = (jnp.dot(att_c, v_stack[ci], preferred_element_type=F32)
                              + _dot_nt(qd[ci], st.astype(BF16)))
        st = st * decay[ci] + jnp.where(block_diag, kv_t[ci], 0.0)
    st_scr[...] = st

    y = _group_rms(o_scr[...], e_ref[...], HEAD_DIM, ng_ref[...]) * _silu(g_out)
    o_ref[...] = y.astype(BF16)


def _gla(gla_in, gla_vt, small, wup_pad, b_gate, norm_g, lb, e64, batch, seq):
    nt = seq // MIX_T
    row = lambda b, t: (b * nt + t, 0)
    return pl.pallas_call(
        _gla_body,
        grid=(batch, nt),
        in_specs=[pl.BlockSpec((MIX_T, 4 * GROUP_W), row),
                  pl.BlockSpec((GROUP_W, MIX_T), lambda b, t: (0, b * nt + t)),
                  pl.BlockSpec((MIX_T, LANES), row),
                  _const_spec((LANES, GROUP_W)),
                  _const_spec((1, GROUP_W)),
                  _const_spec((1, GROUP_W)),
                  _const_spec((GLA_CHUNK, GLA_CHUNK)),
                  _const_spec((GROUP_W, GROUP_W))],
        out_specs=pl.BlockSpec((MIX_T, GROUP_W), row),
        out_shape=jax.ShapeDtypeStruct((batch * seq, GROUP_W), BF16),
        scratch_shapes=[pltpu.VMEM((GROUP_W, GROUP_W), F32),
                        pltpu.VMEM((MIX_T, GROUP_W), F32)],
        compiler_params=_params(2),
        name="gla",
    )(gla_in, gla_vt, small, wup_pad, b_gate, norm_g, lb, e64)


def _attn_body(q_ref, k_ref, v_ref, f_ref, og_ref, o_ref, qs_scr, m_scr, acc_scr, p0_scr, al0_scr, p1_scr, al1_scr):
    tq, tk = ATT_TQ, ATT_TK
    i = pl.program_id(1)
    q0 = pl.multiple_of(i * tq, tq)

    q = q_ref[...]
    lane_head = _head_id((tq, GROUP_W), 1, HEAD_DIM)
    for h in range(N_HEADS):
        qs_scr[h * tq:(h + 1) * tq, :] = jnp.where(lane_head == h, q, jnp.zeros_like(q))
    m_scr[...] = jnp.full_like(m_scr, -jnp.inf)
    acc_scr[...] = jnp.zeros_like(acc_scr)

    f_here = f_ref[:, pl.ds(q0, LANES)][:, 0:1]
    key_lane_head = _head_id((tk, GROUP_W), 1, HEAD_DIM)

    def scores_to_probs(j, masked, p_scr, al_scr):
        c0 = pl.multiple_of(j * tk, tk)
        k = k_ref[pl.ds(c0, tk), :]
        bias = (f_here - f_ref[:, pl.ds(c0, tk)]) * LOG2_E
        if masked:
            keep = (lax.broadcasted_iota(jnp.int32, (tq, tk), 1) + c0
                    <= lax.broadcasted_iota(jnp.int32, (tq, tk), 0) + q0)
        for h in range(N_HEADS):
            hs = slice(h * tq, (h + 1) * tq)
            s = _dot_nt(qs_scr[hs, :], k) + bias[h:h + 1, :]
            if masked:
                s = jnp.where(keep, s, -jnp.inf)
            m_prev = m_scr[hs]
            m_new = jnp.maximum(m_prev, jnp.max(s, axis=-1, keepdims=True))
            al_scr[hs] = jnp.exp2(m_prev - m_new)
            p_scr[hs, :] = jnp.exp2(s - jnp.tile(m_new, (1, tk // LANES))).astype(BF16)
            m_scr[hs] = m_new

    def accumulate(j, p_scr, al_scr):
        c0 = pl.multiple_of(j * tk, tk)
        v = v_ref[pl.ds(c0, tk), :]
        for h in range(N_HEADS):
            hs = slice(h * tq, (h + 1) * tq)
            v_h = jnp.where(key_lane_head == (h + 1) % N_HEADS, jnp.ones_like(v), v)
            acc_scr[hs, :] = (acc_scr[hs, :] * jnp.tile(al_scr[hs], (1, GROUP_W // LANES))
                              + jnp.dot(p_scr[hs, :], v_h, preferred_element_type=F32))

    n_full = (i * tq) // tk
    n_loop = jnp.maximum(n_full - 1, 0)
    buf0, buf1 = (p0_scr, al0_scr), (p1_scr, al1_scr)

    scores_to_probs(0, True, *buf1)

    def pair_body(pair, carry):
        j = 1 + 2 * pair
        accumulate(j - 1, *buf1)
        scores_to_probs(j, False, *buf0)
        accumulate(j, *buf0)
        scores_to_probs(j + 1, False, *buf1)
        return carry

    lax.fori_loop(0, n_loop // 2, pair_body, 0)

    @pl.when(n_loop % 2 == 1)
    def _():
        accumulate(n_full - 2, *buf1)
        scores_to_probs(n_full - 1, False, *buf1)

    @pl.when(n_full >= 1)
    def _():
        accumulate(n_full - 1, *buf1)
        scores_to_probs(n_full, True, *buf1)

    accumulate(n_full, *buf1)

    y = jnp.zeros((tq, GROUP_W), F32)
    for h in range(N_HEADS):
        acc = acc_scr[h * tq:(h + 1) * tq, :]
        row_sum = jnp.max(jnp.where(lane_head == (h + 1) % N_HEADS, acc, -jnp.inf), axis=-1, keepdims=True)
        oh = acc / row_sum
        mine = lane_head == h
        ssq = jnp.sum(jnp.where(mine, oh * oh, 0.0), axis=-1, keepdims=True)
        y = jnp.where(mine, oh * lax.rsqrt(ssq * (1.0 / HEAD_DIM) + EPS), y)
    o_ref[...] = (y * og_ref[...]).astype(BF16)


def _attn(qn, kn, vf, f_cum, out_gain, batch, seq):
    assert ATT_TQ == ATT_TK, "the key-tile pipeline assumes exactly one tile on the causal diagonal"
    nq = seq // ATT_TQ
    rows = N_HEADS * ATT_TQ
    return pl.pallas_call(
        _attn_body,
        grid=(batch, nq),
        in_specs=[pl.BlockSpec((ATT_TQ, GROUP_W), lambda b, i: (b * nq + i, 0)),
                  pl.BlockSpec((seq, GROUP_W), lambda b, i: (b, 0)),
                  pl.BlockSpec((seq, GROUP_W), lambda b, i: (b, 0)),
                  pl.BlockSpec((SUBLANES, seq), lambda b, i: (0, b)),
                  _const_spec((1, GROUP_W))],
        out_specs=pl.BlockSpec((ATT_TQ, GROUP_W), lambda b, i: (b * nq + i, 0)),
        out_shape=jax.ShapeDtypeStruct((batch * seq, GROUP_W), BF16),
        scratch_shapes=[pltpu.VMEM((rows, GROUP_W), BF16),
                        pltpu.VMEM((rows, LANES), F32),
                        pltpu.VMEM((rows, GROUP_W), F32),
                        pltpu.VMEM((rows, ATT_TK), BF16),
                        pltpu.VMEM((rows, LANES), F32),
                        pltpu.VMEM((rows, ATT_TK), BF16),
                        pltpu.VMEM((rows, LANES), F32)],
        compiler_params=_params(2),
        name="fox_attn",
    )(qn, kn, vf, f_cum, out_gain)


def _ssd_body(x_ref, bt_ref, sm_ref, dtr_ref, dtb_ref, dtbr_ref, anr_ref, seldt_ref, dskip_ref, ng_ref, ub_ref,
              o_ref, st_scr):
    t, qc, n = MIX_T, SSM_CHUNK, SSM_STATE

    @pl.when(pl.program_id(1) == 0)
    def _():
        st_scr[...] = jnp.zeros_like(st_scr)

    gate = _silu(x_ref[:, 0:GROUP_W].astype(F32))
    xs = x_ref[:, GROUP_W:2 * GROUP_W].astype(F32)
    dt_col = _softplus(sm_ref[...] + dtb_ref[...])
    xd = (xs * _dot_x01(dt_col, seldt_ref[...], 2)).astype(BF16)

    a_row = _softplus(dtr_ref[...] + dtbr_ref[...]) * anr_ref[...]
    pieces = jnp.dot(jnp.concatenate(_split_bf16(a_row, 3), axis=0), ub_ref[...], preferred_element_type=F32)
    acs_row = pieces[0:SUBLANES] + pieces[SUBLANES:2 * SUBLANES] + pieces[2 * SUBLANES:3 * SUBLANES]

    lane_head = _head_id((qc, GROUP_W), 1, HEAD_DIM)
    tril = (lax.broadcasted_iota(jnp.int32, (qc, qc), 1)
            <= lax.broadcasted_iota(jnp.int32, (qc, qc), 0))
    heads_per_group = N_HEADS // SSM_GROUPS
    state_block = (_head_id((N_HEADS * n, GROUP_W), 0, n) == _head_id((N_HEADS * n, GROUP_W), 1, HEAD_DIM))

    n_chunks = t // qc
    rows = [slice(ci * qc, (ci + 1) * qc) for ci in range(n_chunks)]
    bt_grp = [[bt_ref[g * n:(g + 1) * n, r] for g in range(SSM_GROUPS)] for r in rows]
    c_grp = [[x_ref[r, 2 * GROUP_W + (SSM_GROUPS + g) * n:2 * GROUP_W + (SSM_GROUPS + g + 1) * n]
              for g in range(SSM_GROUPS)] for r in rows]
    cb = [[jnp.dot(c_grp[ci][g], bt_grp[ci][g], preferred_element_type=F32) for g in range(SSM_GROUPS)]
          for ci in range(n_chunks)]

    lhs_parts, decays, incs = [], [], []
    for ci, r in enumerate(rows):
        scores, c_scaled, b_decayed, e_last = [], [], [], []
        for h in range(N_HEADS):
            g = h // heads_per_group
            ac_t = jnp.broadcast_to(acs_row[h:h + 1, r], (qc, qc))
            ac = ac_t.T
            lmat = jnp.exp(jnp.where(tril, ac - ac_t, -jnp.inf))
            scores.append((cb[ci][g] * lmat).astype(BF16))
            a_last = ac[qc - 1:qc, :]
            e_last.append(jnp.exp(a_last))
            b_decayed.append((bt_grp[ci][g].astype(F32) * jnp.exp(a_last - ac_t[0:1, :])).astype(BF16))
            c_scaled.append((c_grp[ci][g].astype(F32) * jnp.exp(ac)).astype(BF16))
        lhs_parts.append(jnp.concatenate(scores + c_scaled, axis=1))
        decays.append(jnp.concatenate([jnp.broadcast_to(jnp.tile(e, (1, GROUP_W // LANES)), (n, GROUP_W))
                                       for e in e_last], axis=0))
        incs.append(jnp.dot(jnp.concatenate(b_decayed, axis=0), xd[r], preferred_element_type=F32))

    st = st_scr[...]
    for ci, r in enumerate(rows):
        xd_c = xd[r]
        zero = jnp.zeros_like(xd_c)
        xd_heads = [jnp.where(lane_head == h, xd_c, zero) for h in range(N_HEADS)]
        rhs = jnp.concatenate(xd_heads + [st.astype(BF16)], axis=0)
        y = jnp.dot(lhs_parts[ci], rhs, preferred_element_type=F32)
        st = st * decays[ci] + jnp.where(state_block, incs[ci], 0.0)

        y = (y + xs[r] * dskip_ref[...]) * gate[r]
        for g in range(SSM_GROUPS):
            gs = slice(g * LANES, (g + 1) * LANES)
            yg = y[:, gs]
            ms = jnp.mean(yg * yg, axis=-1, keepdims=True)
            o_ref[r, gs] = (yg * lax.rsqrt(ms + EPS) * ng_ref[:, gs]).astype(BF16)
    st_scr[...] = st


def _ssd(ssd_in, ssd_bt, small, dt_rows, dtb, dtb_col, a_neg_col, seldt, d_skip, norm_g, utri_chunks, batch, seq):
    nt = seq // MIX_T
    row = lambda b, t: (b * nt + t, 0)
    return pl.pallas_call(
        _ssd_body,
        grid=(batch, nt),
        in_specs=[pl.BlockSpec((MIX_T, GROUP_W + SSM_XBC), row),
                  pl.BlockSpec((GROUP_W, MIX_T), lambda b, t: (0, b * nt + t)),
                  pl.BlockSpec((MIX_T, LANES), row),
                  pl.BlockSpec((SUBLANES, MIX_T), lambda b, t: (0, b * nt + t)),
                  _const_spec((1, LANES)),
                  _const_spec((SUBLANES, 1)),
                  _const_spec((SUBLANES, 1)),
                  _const_spec((LANES, GROUP_W)),
                  _const_spec((1, GROUP_W)),
                  _const_spec((1, GROUP_W)),
                  _const_spec((MIX_T, MIX_T))],
        out_specs=pl.BlockSpec((MIX_T, GROUP_W), row),
        out_shape=jax.ShapeDtypeStruct((batch * seq, GROUP_W), BF16),
        scratch_shapes=[pltpu.VMEM((N_HEADS * SSM_STATE, GROUP_W), F32)],
        compiler_params=_params(2),
        name="ssd",
    )(ssd_in, ssd_bt, small, dt_rows, dtb, dtb_col, a_neg_col, seldt, d_skip, norm_g, utri_chunks)


def _block_diag_ones(size, block):
    idx = np.arange(size) // block
    return jnp.asarray(idx[:, None] == idx[None, :], BF16)


def _chunk_lower_tri(size, block):
    i = np.arange(size)
    same = (i[:, None] // block) == (i[None, :] // block)
    return jnp.asarray(same & (i[None, :] <= i[:, None]), BF16)


def _lane_spread(src_lane0, width):
    m = np.zeros((LANES, N_HEADS * width))
    for h in range(N_HEADS):
        m[src_lane0 + h, h * width:(h + 1) * width] = 1.0
    return jnp.asarray(m, BF16)


def _row(v):
    return v.reshape(1, -1).astype(F32)


def _lane_slot(values, offset, width):
    return jnp.zeros((1, width), F32).at[0, offset:offset + values.shape[0]].set(values.astype(F32))


def _sublane_col(values):
    return jnp.zeros((SUBLANES, 1), F32).at[0:values.shape[0], 0].set(values.astype(F32))


def _regroup_w_in(w_in):
    gla_end = 4 * GROUP_W
    fox0 = GLA_COLS
    fox_end = fox0 + 3 * GROUP_W
    ssm0 = GLA_COLS + FOX_COLS
    ssm_end = ssm0 + GROUP_W + SSM_XBC
    sc0 = GLA_COLS + FOX_COLS + SSM_COLS
    w_main = jnp.concatenate([w_in[:, 0:gla_end], w_in[:, fox0:fox_end],
                              w_in[:, ssm0:ssm_end], w_in[:, sc0:]], axis=1).astype(BF16)
    w_small = jnp.zeros((D_MODEL, LANES), F32)
    w_small = w_small.at[:, SMALL_LR:SMALL_LR + GLA_GATE_RANK].set(w_in[:, gla_end:GLA_COLS])
    w_small = w_small.at[:, SMALL_F:SMALL_F + N_HEADS].set(w_in[:, fox_end:ssm0])
    w_small = w_small.at[:, SMALL_DT:SMALL_DT + N_HEADS].set(w_in[:, ssm_end:sc0])
    return w_main, w_small.astype(BF16)


def kernel(x, ffn1_norm, ffn1_w_gate, ffn1_w_up, ffn1_w_down, mix_norm, w_in, gla_w_gate_up, gla_b_gate, gla_norm, fox_b_forget, fox_q_norm, fox_k_norm, fox_out_norm, ssm_conv_w, ssm_conv_b, ssm_dt_bias, ssm_A_log, ssm_D, ssm_norm, sc_conv_w, sc_out_norm, w_out, ffn2_norm, ffn2_w_gate, ffn2_w_up, ffn2_w_down):
    batch, seq, _ = x.shape
    depth = w_in.shape[0]
    assert seq % MIX_T == 0 and seq % PROJ_TM == 0 and seq % ATT_TK == 0 and seq % ATT_TQ == 0
    assert (batch * seq) % FFN_TM == 0

    e64 = _block_diag_ones(GROUP_W, HEAD_DIM)
    lb_gla = _chunk_lower_tri(GLA_CHUNK, GLA_CHUNK)
    utri_t = jnp.asarray(np.triu(np.ones((PROJ_TM, PROJ_TM))), BF16)
    ub_ssd = _chunk_lower_tri(MIX_T, SSM_CHUNK).T
    seldt = _lane_spread(SMALL_DT, HEAD_DIM)

    w1g, w1u, w1d = _to_bf16(ffn1_w_gate), _to_bf16(ffn1_w_up), _to_bf16(ffn1_w_down)
    w2g, w2u, w2d = _to_bf16(ffn2_w_gate), _to_bf16(ffn2_w_up), _to_bf16(ffn2_w_down)
    wo = _to_bf16(w_out)

    xf = x.reshape(batch * seq, D_MODEL)
    for l in range(depth):
        xf = _ffn(xf, _row(ffn1_norm[l]), w1g[l], w1u[l], w1d[l])

        w_main, w_small = _regroup_w_in(w_in[l])
        gla_in, gla_vt, qn, kn, vf, f_cum, ssd_in, ssd_bt, yd, small, dt_rows = _inproj(
            xf, _row(mix_norm[l]), w_main, w_small, _sublane_col(fox_b_forget[l]),
            _row(jnp.tile(fox_q_norm[l], N_HEADS)), _row(jnp.tile(fox_k_norm[l], N_HEADS)),
            e64, utri_t, ssm_conv_w[l].astype(F32), _row(ssm_conv_b[l]),
            sc_conv_w[l].astype(F32), _row(sc_out_norm[l]), seq)

        wup_pad = jnp.zeros((LANES, GROUP_W), F32).at[SMALL_LR:SMALL_LR + GLA_GATE_RANK].set(
            gla_w_gate_up[l]).astype(BF16)
        ya = _gla(gla_in, gla_vt, small, wup_pad, _row(gla_b_gate[l]), _row(gla_norm[l]), lb_gla, e64, batch, seq)

        yb = _attn(qn, kn, vf, f_cum, _row(fox_out_norm[l]), batch, seq)

        a_neg = -jnp.exp(ssm_A_log[l].astype(F32))
        yc = _ssd(ssd_in, ssd_bt, small, dt_rows, _lane_slot(ssm_dt_bias[l], SMALL_DT, LANES),
                  _sublane_col(ssm_dt_bias[l]), _sublane_col(a_neg), seldt,
                  _row(jnp.repeat(ssm_D[l], HEAD_DIM)), _row(ssm_norm[l]), ub_ssd, batch, seq)

        xf = _ffn(xf, _row(ffn2_norm[l]), w2g[l], w2u[l], w2d[l], mixers=(ya, yb, yc, yd), w_out=wo[l])
    return xf.reshape(batch, seq, D_MODEL)
```

```python
import functools

import numpy as np
import jax
import jax.numpy as jnp
from jax import lax
from jax.experimental import pallas as pl
from jax.experimental.pallas import tpu as pltpu

F32, BF16 = jnp.float32, jnp.bfloat16

D_MODEL = 1024
GROUP_W = 256
HEAD_DIM = 64
N_HEADS = GROUP_W // HEAD_DIM
GLA_GATE_RANK = 16
GLA_GATE_NORMALIZER = 16.0
GLA_CHUNK = 64
SSM_GROUPS = 2
SSM_STATE = 128
SSM_CONV = 4
SSM_CHUNK = 128
SSM_XBC = GROUP_W + 2 * SSM_GROUPS * SSM_STATE
SC_GROUPS = 4
SC_CONV = 3
D_FF = 2816
EPS = 1e-6
LOG2_E = 1.4426950408889634

GLA_COLS = 4 * GROUP_W + GLA_GATE_RANK
FOX_COLS = 3 * GROUP_W + N_HEADS
SSM_COLS = GROUP_W + SSM_XBC + N_HEADS

COL_GLA = 0
COL_FOX = COL_GLA + 4 * GROUP_W
COL_SSD = COL_FOX + 3 * GROUP_W
COL_SC = COL_SSD + GROUP_W + SSM_XBC
MAIN_COLS = COL_SC + 3 * GROUP_W

LANES = 128
SUBLANES = 8
SMALL_LR = 0
SMALL_F = 16
SMALL_DT = 24

FFN_TM = 1024
FF_CHUNK = 256
PROJ_TM = 1024
MIX_T = 1024
ATT_TQ = 512
ATT_TK = 512
CAST_ROW_BLOCKS = 4
HALO = 8

VMEM_LIMIT = 56 * 1024 * 1024


def _const_spec(shape):
    nd = len(shape)
    return pl.BlockSpec(shape, lambda *_: (0,) * nd, pipeline_mode=pl.Buffered(1))


def _params(n_axes):
    return pltpu.CompilerParams(dimension_semantics=("arbitrary",) * n_axes,
                                vmem_limit_bytes=VMEM_LIMIT)


def _split_bf16(x, pieces):
    out, r = [], x
    for i in range(pieces):
        p = r.astype(BF16)
        out.append(p)
        if i + 1 < pieces:
            r = r - p.astype(F32)
    return out


def _dot_x01(x, m01, pieces):
    acc = None
    for p in _split_bf16(x, pieces):
        d = jnp.dot(p, m01, preferred_element_type=F32)
        acc = d if acc is None else acc + d
    return acc


def _dot_01x(m01, x, pieces):
    acc = None
    for p in _split_bf16(x, pieces):
        d = jnp.dot(m01, p, preferred_element_type=F32)
        acc = d if acc is None else acc + d
    return acc


def _dot_nt(a, b):
    return lax.dot_general(a, b, (((1,), (1,)), ((), ())), preferred_element_type=F32)


def _log_sigmoid(x):
    return jnp.minimum(x, 0.0) - jnp.log1p(jnp.exp(-jnp.abs(x)))


def _softplus(x):
    return jnp.maximum(x, 0.0) + jnp.log1p(jnp.exp(-jnp.abs(x)))


def _silu(x):
    return x * jax.nn.sigmoid(x)


def _rms_rows(x, gain):
    ms = jnp.mean(x * x, axis=-1, keepdims=True)
    return x * lax.rsqrt(ms + EPS) * gain


def _group_rms(x, ones_blocks, width, gain):
    ssq = _dot_x01(x * x, ones_blocks, 1)
    return x * lax.rsqrt(ssq * (1.0 / width) + EPS) * gain


def _head_id(shape, axis, width):
    return lax.broadcasted_iota(jnp.int32, shape, axis) // width


def _swiglu_half_step(x, g_ref, wg_ref, wu_ref, wd_ref, a_scr):
    h = _rms_rows(x, g_ref[...]).astype(BF16)
    for j in range(D_FF // FF_CHUNK):
        sl = slice(j * FF_CHUNK, (j + 1) * FF_CHUNK)
        gate = jnp.dot(h, wg_ref[:, sl], preferred_element_type=F32)
        up = jnp.dot(h, wu_ref[:, sl], preferred_element_type=F32)
        a_scr[:, sl] = (_silu(gate) * up).astype(BF16)
    return x + 0.5 * jnp.dot(a_scr[...], wd_ref[...], preferred_element_type=F32)


def _ffn_body(x_ref, g_ref, wg_ref, wu_ref, wd_ref, o_ref, a_scr):
    o_ref[...] = _swiglu_half_step(x_ref[...], g_ref, wg_ref, wu_ref, wd_ref, a_scr)


def _mix_ffn_body(x_ref, ya_ref, yb_ref, yc_ref, yd_ref, wo_ref, g_ref, wg_ref, wu_ref, wd_ref, o_ref, a_scr):
    x = x_ref[...]
    for i, y_ref in enumerate((ya_ref, yb_ref, yc_ref, yd_ref)):
        x = x + jnp.dot(y_ref[...], wo_ref[i * GROUP_W:(i + 1) * GROUP_W, :], preferred_element_type=F32)
    o_ref[...] = _swiglu_half_step(x, g_ref, wg_ref, wu_ref, wd_ref, a_scr)


def _ffn(x, gain, wg, wu, wd, mixers=None, w_out=None):
    m = x.shape[0]
    row = lambda i: (i, 0)
    ffn_specs = [_const_spec((1, D_MODEL)),
                 _const_spec((D_MODEL, D_FF)),
                 _const_spec((D_MODEL, D_FF)),
                 _const_spec((D_FF, D_MODEL))]
    if mixers is None:
        body, name, lead, lead_specs = _ffn_body, "ffn", (), []
    else:
        body, name, lead = _mix_ffn_body, "mix_ffn", tuple(mixers) + (w_out,)
        lead_specs = [pl.BlockSpec((FFN_TM, GROUP_W), row)] * len(mixers) + [_const_spec((4 * GROUP_W, D_MODEL))]
    return pl.pallas_call(
        body,
        grid=(m // FFN_TM,),
        in_specs=[pl.BlockSpec((FFN_TM, D_MODEL), row)] + lead_specs + ffn_specs,
        out_specs=pl.BlockSpec((FFN_TM, D_MODEL), row),
        out_shape=jax.ShapeDtypeStruct((m, D_MODEL), F32),
        scratch_shapes=[pltpu.VMEM((FFN_TM, D_FF), BF16)],
        compiler_params=_params(1),
        name=name,
    )(x, *lead, gain, wg, wu, wd)


def _cast_body(w_ref, o_ref):
    o_ref[...] = w_ref[...].astype(BF16)


def _to_bf16(w):
    depth, rows, cols = w.shape
    block_rows = rows // CAST_ROW_BLOCKS
    assert block_rows * CAST_ROW_BLOCKS == rows and block_rows % 16 == 0
    spec = pl.BlockSpec((1, block_rows, cols), lambda l, i: (l, i, 0))
    return pl.pallas_call(
        _cast_body,
        grid=(depth, CAST_ROW_BLOCKS),
        in_specs=[spec],
        out_specs=spec,
        out_shape=jax.ShapeDtypeStruct(w.shape, BF16),
        compiler_params=_params(2),
        name="cast_bf16",
    )(w)


def _inproj_body(x_ref, g_ref, wm_ref, ws_ref, bf_ref, qg_ref, kg_ref, e_ref, ut_ref,
                 cw_ref, cb_ref, scw_ref, scg_ref,
                 gla_ref, gvt_ref, qn_ref, kn_ref, vf_ref, f_ref, ssd_ref, sbt_ref, yd_ref, small_ref, dtr_ref,
                 xb_scr, u_scr, f_carry, *, tiles_per_seq):
    t = PROJ_TM

    @pl.when(pl.program_id(0) % tiles_per_seq == 0)
    def _():
        xb_scr[0:HALO, :] = jnp.zeros((HALO, SSM_XBC), F32)
        u_scr[0:HALO, :] = jnp.zeros((HALO, GROUP_W), F32)
        f_carry[...] = jnp.zeros_like(f_carry)

    h = _rms_rows(x_ref[...], g_ref[...]).astype(BF16)

    def proj(col):
        return jnp.dot(h, wm_ref[:, col:col + GROUP_W], preferred_element_type=F32)

    n_xbc = SSM_XBC // GROUP_W
    p_xbc = [proj(COL_SSD + GROUP_W + c * GROUP_W) for c in range(n_xbc)]
    p_sc = [proj(COL_SC + c * GROUP_W) for c in range(3)]
    p_fox = [proj(COL_FOX + c * GROUP_W) for c in range(3)]
    small = jnp.dot(h, ws_ref[...], preferred_element_type=F32)
    p_z = proj(COL_SSD)
    p_gla = [proj(COL_GLA + c * GROUP_W) for c in range(4)]

    for c in range(4):
        gla_ref[:, c * GROUP_W:(c + 1) * GROUP_W] = p_gla[c].astype(BF16)
    gvt_ref[...] = p_gla[2].T.astype(BF16)

    vf_ref[...] = p_fox[2].astype(BF16)
    q_scale = HEAD_DIM ** -0.5 * LOG2_E

    ssd_ref[:, 0:GROUP_W] = p_z.astype(BF16)
    for c in range(n_xbc):
        cols = slice(c * GROUP_W, (c + 1) * GROUP_W)
        xb_scr[HALO:HALO + t, cols] = p_xbc[c]
    for c in range(n_xbc):
        cols = slice(c * GROUP_W, (c + 1) * GROUP_W)
        conv = cb_ref[:, cols]
        for kk in range(SSM_CONV):
            conv = conv + cw_ref[kk:kk + 1, cols] * xb_scr[pl.ds(HALO - (SSM_CONV - 1) + kk, t), cols]
        act = _silu(conv)
        ssd_ref[:, GROUP_W + c * GROUP_W:2 * GROUP_W + c * GROUP_W] = act.astype(BF16)
        if c == 1:
            sbt_ref[...] = act.T.astype(BF16)
    xb_scr[0:HALO, :] = xb_scr[t:t + HALO, :]

    u_scr[HALO:HALO + t, :] = p_sc[1] * p_sc[2]
    conv = jnp.zeros((t, GROUP_W), F32)
    for kk in range(SC_CONV):
        conv = conv + scw_ref[kk:kk + 1, :] * u_scr[pl.ds(HALO - (SC_CONV - 1) + kk, t), :]
    u_scr[0:HALO, :] = u_scr[t:t + HALO, :]

    qn_ref[...] = (_group_rms(p_fox[0], e_ref[...], HEAD_DIM, qg_ref[...]) * q_scale).astype(BF16)
    kn_ref[...] = _group_rms(p_fox[1], e_ref[...], HEAD_DIM, kg_ref[...]).astype(BF16)
    yd_ref[...] = _group_rms(p_sc[0] * conv, e_ref[...], GROUP_W // SC_GROUPS, scg_ref[...]).astype(BF16)

    small_ref[...] = small
    small_t = small.T
    dtr_ref[...] = small_t[SMALL_DT:SMALL_DT + SUBLANES, :]
    log_f = _log_sigmoid(small_t[SMALL_F:SMALL_F + SUBLANES, :] + bf_ref[...])
    parts = jnp.dot(jnp.concatenate(_split_bf16(log_f, 3), axis=0), ut_ref[...], preferred_element_type=F32)
    f_ref[...] = (parts[0:SUBLANES] + parts[SUBLANES:2 * SUBLANES] + parts[2 * SUBLANES:3 * SUBLANES]
                  + f_carry[:, 0:1])
    f_carry[...] = f_carry[...] + jnp.sum(log_f, axis=-1, keepdims=True)


def _inproj(x, gain, w_main, w_small, b_forget, q_gain, k_gain, e64, utri, conv_w, conv_b,
            sc_conv_w, sc_gain, seq):
    m = x.shape[0]
    t = PROJ_TM
    row = lambda i: (i, 0)
    col = lambda i: (0, i)
    out_shape = [jax.ShapeDtypeStruct((m, 4 * GROUP_W), BF16),
                 jax.ShapeDtypeStruct((GROUP_W, m), BF16),
                 jax.ShapeDtypeStruct((m, GROUP_W), BF16),
                 jax.ShapeDtypeStruct((m, GROUP_W), BF16),
                 jax.ShapeDtypeStruct((m, GROUP_W), BF16),
                 jax.ShapeDtypeStruct((SUBLANES, m), F32),
                 jax.ShapeDtypeStruct((m, GROUP_W + SSM_XBC), BF16),
                 jax.ShapeDtypeStruct((GROUP_W, m), BF16),
                 jax.ShapeDtypeStruct((m, GROUP_W), BF16),
                 jax.ShapeDtypeStruct((m, LANES), F32),
                 jax.ShapeDtypeStruct((SUBLANES, m), F32)]
    out_specs = [pl.BlockSpec((t, 4 * GROUP_W), row),
                 pl.BlockSpec((GROUP_W, t), col),
                 pl.BlockSpec((t, GROUP_W), row),
                 pl.BlockSpec((t, GROUP_W), row),
                 pl.BlockSpec((t, GROUP_W), row),
                 pl.BlockSpec((SUBLANES, t), col),
                 pl.BlockSpec((t, GROUP_W + SSM_XBC), row),
                 pl.BlockSpec((GROUP_W, t), col),
                 pl.BlockSpec((t, GROUP_W), row),
                 pl.BlockSpec((t, LANES), row),
                 pl.BlockSpec((SUBLANES, t), col)]
    return pl.pallas_call(
        functools.partial(_inproj_body, tiles_per_seq=seq // t),
        grid=(m // t,),
        in_specs=[pl.BlockSpec((t, D_MODEL), row),
                  _const_spec((1, D_MODEL)),
                  _const_spec((D_MODEL, MAIN_COLS)),
                  _const_spec((D_MODEL, LANES)),
                  _const_spec((SUBLANES, 1)),
                  _const_spec((1, GROUP_W)),
                  _const_spec((1, GROUP_W)),
                  _const_spec((GROUP_W, GROUP_W)),
                  _const_spec((t, t)),
                  _const_spec((SSM_CONV, SSM_XBC)),
                  _const_spec((1, SSM_XBC)),
                  _const_spec((SC_CONV, GROUP_W)),
                  _const_spec((1, GROUP_W))],
        out_specs=out_specs,
        out_shape=out_shape,
        scratch_shapes=[pltpu.VMEM((t + HALO, SSM_XBC), F32),
                        pltpu.VMEM((t + HALO, GROUP_W), F32),
                        pltpu.VMEM((SUBLANES, LANES), F32)],
        compiler_params=_params(1),
        name="inproj",
    )(x, gain, w_main, w_small, b_forget, q_gain, k_gain, e64, utri, conv_w, conv_b, sc_conv_w, sc_gain)


def _gla_body(x_ref, vt_ref, sm_ref, wup_ref, bg_ref, ng_ref, lb_ref, e_ref, o_ref, st_scr, o_scr):
    t = MIX_T

    @pl.when(pl.program_id(1) == 0)
    def _():
        st_scr[...] = jnp.zeros_like(st_scr)

    q = x_ref[:, 0:GROUP_W].astype(F32) * (HEAD_DIM ** -0.5)
    k = x_ref[:, GROUP_W:2 * GROUP_W].astype(F32)
    v = x_ref[:, 2 * GROUP_W:3 * GROUP_W]
    g_out = x_ref[:, 3 * GROUP_W:4 * GROUP_W].astype(F32)

    z = jnp.dot(sm_ref[...].astype(BF16), wup_ref[...], preferred_element_type=F32) + bg_ref[...]
    log_a = _log_sigmoid(z) * (1.0 / GLA_GATE_NORMALIZER)
    c = GLA_CHUNK
    head_rows = _head_id((GROUP_W, GROUP_W), 0, HEAD_DIM)
    head_cols = _head_id((GROUP_W, GROUP_W), 1, HEAD_DIM)
    block_diag = head_rows == head_cols
    lane_head = _head_id((c, GROUP_W), 1, HEAD_DIM)
    row_id = lax.broadcasted_iota(jnp.int32, (c, GROUP_W), 0)
    key_id = lax.broadcasted_iota(jnp.int32, (c, GROUP_W), 1) % HEAD_DIM
    causal = key_id <= row_id

    n_chunks = t // c
    rows = [slice(ci * c, (ci + 1) * c) for ci in range(n_chunks)]
    b_chunks = [_dot_01x(lb_ref[...], log_a[r], 3) for r in rows]
    qd, ke, v_stack, att, decay = [], [], [], [], []
    for r, bc in zip(rows, b_chunks):
        bl = bc[c - 1:c, :]
        decay.append(jnp.exp(bl))
        qd.append((q[r] * jnp.exp(bc)).astype(BF16))
        kd = (k[r] * jnp.exp(-bc)).astype(BF16)
        ke.append((k[r] * jnp.exp(bl - bc)).astype(BF16))
        zero = jnp.zeros_like(kd)
        k_stack = jnp.concatenate([jnp.where(lane_head == h, kd, zero) for h in range(N_HEADS)], axis=0)
        v_stack.append(jnp.concatenate([jnp.where(lane_head == h, v[r], zero) for h in range(N_HEADS)], axis=0))
        att.append(_dot_nt(qd[-1], k_stack))

    kv_t = []
    for ci in range(n_chunks):
        half = jnp.zeros_like(ke[ci])
        ke_pair = jnp.concatenate([ke[ci], half] if ci % 2 == 0 else [half, ke[ci]], axis=0)
        kv_t.append(jnp.dot(vt_ref[:, (ci // 2) * LANES:(ci // 2 + 1) * LANES], ke_pair,
                            preferred_element_type=F32))

    st = st_scr[...]
    for ci in range(n_chunks):
        att_c = jnp.where(causal, att[ci], 0.0).astype(BF16)
        o_scr[rows[ci], :] = (jnp.dot(att_c, v_stack[ci], preferred_element_type=F32)
                              + _dot_nt(qd[ci], st.astype(BF16)))
        st = st * decay[ci] + jnp.where(block_diag, kv_t[ci], 0.0)
    st_scr[...] = st

    y = _group_rms(o_scr[...], e_ref[...], HEAD_DIM, ng_ref[...]) * _silu(g_out)
    o_ref[...] = y.astype(BF16)


def _gla(gla_in, gla_vt, small, wup_pad, b_gate, norm_g, lb, e64, batch, seq):
    nt = seq // MIX_T
    row = lambda b, t: (b * nt + t, 0)
    return pl.pallas_call(
        _gla_body,
        grid=(batch, nt),
        in_specs=[pl.BlockSpec((MIX_T, 4 * GROUP_W), row),
                  pl.BlockSpec((GROUP_W, MIX_T), lambda b, t: (0, b * nt + t)),
                  pl.BlockSpec((MIX_T, LANES), row),
                  _const_spec((LANES, GROUP_W)),
                  _const_spec((1, GROUP_W)),
                  _const_spec((1, GROUP_W)),
                  _const_spec((GLA_CHUNK, GLA_CHUNK)),
                  _const_spec((GROUP_W, GROUP_W))],
        out_specs=pl.BlockSpec((MIX_T, GROUP_W), row),
        out_shape=jax.ShapeDtypeStruct((batch * seq, GROUP_W), BF16),
        scratch_shapes=[pltpu.VMEM((GROUP_W, GROUP_W), F32),
                        pltpu.VMEM((MIX_T, GROUP_W), F32)],
        compiler_params=_params(2),
        name="gla",
    )(gla_in, gla_vt, small, wup_pad, b_gate, norm_g, lb, e64)


def _attn_body(q_ref, k_ref, v_ref, f_ref, og_ref, o_ref, qs_scr, m_scr, acc_scr, p0_scr, al0_scr, p1_scr, al1_scr):
    tq, tk = ATT_TQ, ATT_TK
    i = pl.program_id(1)
    q0 = pl.multiple_of(i * tq, tq)

    q = q_ref[...]
    lane_head = _head_id((tq, GROUP_W), 1, HEAD_DIM)
    for h in range(N_HEADS):
        qs_scr[h * tq:(h + 1) * tq, :] = jnp.where(lane_head == h, q, jnp.zeros_like(q))
    m_scr[...] = jnp.full_like(m_scr, -jnp.inf)
    acc_scr[...] = jnp.zeros_like(acc_scr)

    f_here = f_ref[:, pl.ds(q0, LANES)][:, 0:1]
    key_lane_head = _head_id((tk, GROUP_W), 1, HEAD_DIM)

    def scores_to_probs(j, masked, p_scr, al_scr):
        c0 = pl.multiple_of(j * tk, tk)
        k = k_ref[pl.ds(c0, tk), :]
        bias = (f_here - f_ref[:, pl.ds(c0, tk)]) * LOG2_E
        if masked:
            keep = (lax.broadcasted_iota(jnp.int32, (tq, tk), 1) + c0
                    <= lax.broadcasted_iota(jnp.int32, (tq, tk), 0) + q0)
        for h in range(N_HEADS):
            hs = slice(h * tq, (h + 1) * tq)
            s = _dot_nt(qs_scr[hs, :], k) + bias[h:h + 1, :]
            if masked:
                s = jnp.where(keep, s, -jnp.inf)
            m_prev = m_scr[hs]
            m_new = jnp.maximum(m_prev, jnp.max(s, axis=-1, keepdims=True))
            al_scr[hs] = jnp.exp2(m_prev - m_new)
            p_scr[hs, :] = jnp.exp2(s - jnp.tile(m_new, (1, tk // LANES))).astype(BF16)
            m_scr[hs] = m_new

    def accumulate(j, p_scr, al_scr):
        c0 = pl.multiple_of(j * tk, tk)
        v = v_ref[pl.ds(c0, tk), :]
        for h in range(N_HEADS):
            hs = slice(h * tq, (h + 1) * tq)
            v_h = jnp.where(key_lane_head == (h + 1) % N_HEADS, jnp.ones_like(v), v)
            acc_scr[hs, :] = (acc_scr[hs, :] * jnp.tile(al_scr[hs], (1, GROUP_W // LANES))
                              + jnp.dot(p_scr[hs, :], v_h, preferred_element_type=F32))

    n_full = (i * tq) // tk
    buf0, buf1 = (p0_scr, al0_scr), (p1_scr, al1_scr)

    scores_to_probs(n_full, True, *buf1)
    prev = lambda j: jnp.where(j == 0, n_full, j - 1)

    def pair_body(pair, carry):
        j = 2 * pair
        accumulate(prev(j), *buf1)
        scores_to_probs(j, False, *buf0)
        accumulate(j, *buf0)
        scores_to_probs(j + 1, False, *buf1)
        return carry

    lax.fori_loop(0, n_full // 2, pair_body, 0)

    @pl.when(n_full % 2 == 1)
    def _():
        accumulate(prev(n_full - 1), *buf1)
        scores_to_probs(n_full - 1, False, *buf1)

    accumulate(jnp.where(n_full == 0, 0, n_full - 1), *buf1)

    y = jnp.zeros((tq, GROUP_W), F32)
    for h in range(N_HEADS):
        acc = acc_scr[h * tq:(h + 1) * tq, :]
        row_sum = jnp.max(jnp.where(lane_head == (h + 1) % N_HEADS, acc, -jnp.inf), axis=-1, keepdims=True)
        oh = acc / row_sum
        mine = lane_head == h
        ssq = jnp.sum(jnp.where(mine, oh * oh, 0.0), axis=-1, keepdims=True)
        y = jnp.where(mine, oh * lax.rsqrt(ssq * (1.0 / HEAD_DIM) + EPS), y)
    o_ref[...] = (y * og_ref[...]).astype(BF16)


def _attn(qn, kn, vf, f_cum, out_gain, batch, seq):
    assert ATT_TQ == ATT_TK, "the key-tile pipeline assumes exactly one tile on the causal diagonal"
    nq = seq // ATT_TQ
    rows = N_HEADS * ATT_TQ
    return pl.pallas_call(
        _attn_body,
        grid=(batch, nq),
        in_specs=[pl.BlockSpec((ATT_TQ, GROUP_W), lambda b, i: (b * nq + i, 0)),
                  pl.BlockSpec((seq, GROUP_W), lambda b, i: (b, 0)),
                  pl.BlockSpec((seq, GROUP_W), lambda b, i: (b, 0)),
                  pl.BlockSpec((SUBLANES, seq), lambda b, i: (0, b)),
                  _const_spec((1, GROUP_W))],
        out_specs=pl.BlockSpec((ATT_TQ, GROUP_W), lambda b, i: (b * nq + i, 0)),
        out_shape=jax.ShapeDtypeStruct((batch * seq, GROUP_W), BF16),
        scratch_shapes=[pltpu.VMEM((rows, GROUP_W), BF16),
                        pltpu.VMEM((rows, LANES), F32),
                        pltpu.VMEM((rows, GROUP_W), F32),
                        pltpu.VMEM((rows, ATT_TK), BF16),
                        pltpu.VMEM((rows, LANES), F32),
                        pltpu.VMEM((rows, ATT_TK), BF16),
                        pltpu.VMEM((rows, LANES), F32)],
        compiler_params=_params(2),
        name="fox_attn",
    )(qn, kn, vf, f_cum, out_gain)


def _ssd_body(x_ref, bt_ref, sm_ref, dtr_ref, dtb_ref, dtbr_ref, anr_ref, seldt_ref, dskip_ref, ng_ref, ub_ref,
              o_ref, st_scr):
    t, qc, n = MIX_T, SSM_CHUNK, SSM_STATE

    @pl.when(pl.program_id(1) == 0)
    def _():
        st_scr[...] = jnp.zeros_like(st_scr)

    gate = _silu(x_ref[:, 0:GROUP_W].astype(F32))
    xs = x_ref[:, GROUP_W:2 * GROUP_W].astype(F32)
    dt_col = _softplus(sm_ref[...] + dtb_ref[...])
    xd = (xs * _dot_x01(dt_col, seldt_ref[...], 2)).astype(BF16)

    a_row = _softplus(dtr_ref[...] + dtbr_ref[...]) * anr_ref[...]
    pieces = jnp.dot(jnp.concatenate(_split_bf16(a_row, 3), axis=0), ub_ref[...], preferred_element_type=F32)
    acs_row = pieces[0:SUBLANES] + pieces[SUBLANES:2 * SUBLANES] + pieces[2 * SUBLANES:3 * SUBLANES]

    lane_head = _head_id((qc, GROUP_W), 1, HEAD_DIM)
    tril = (lax.broadcasted_iota(jnp.int32, (qc, qc), 1)
            <= lax.broadcasted_iota(jnp.int32, (qc, qc), 0))
    heads_per_group = N_HEADS // SSM_GROUPS
    state_block = (_head_id((N_HEADS * n, GROUP_W), 0, n) == _head_id((N_HEADS * n, GROUP_W), 1, HEAD_DIM))

    n_chunks = t // qc
    rows = [slice(ci * qc, (ci + 1) * qc) for ci in range(n_chunks)]
    bt_grp = [[bt_ref[g * n:(g + 1) * n, r] for g in range(SSM_GROUPS)] for r in rows]
    c_grp = [[x_ref[r, 2 * GROUP_W + (SSM_GROUPS + g) * n:2 * GROUP_W + (SSM_GROUPS + g + 1) * n]
              for g in range(SSM_GROUPS)] for r in rows]
    cb = [[jnp.dot(c_grp[ci][g], bt_grp[ci][g], preferred_element_type=F32) for g in range(SSM_GROUPS)]
          for ci in range(n_chunks)]

    lhs_parts, decays, incs = [], [], []
    for ci, r in enumerate(rows):
        scores, c_scaled, b_decayed, e_last = [], [], [], []
        for h in range(N_HEADS):
            g = h // heads_per_group
            ac_t = jnp.broadcast_to(acs_row[h:h + 1, r], (qc, qc))
            ac = ac_t.T
            lmat = jnp.exp(jnp.where(tril, ac - ac_t, -jnp.inf))
            scores.append((cb[ci][g] * lmat).astype(BF16))
            a_last = ac[qc - 1:qc, :]
            e_last.append(jnp.exp(a_last))
            b_decayed.append((bt_grp[ci][g].astype(F32) * jnp.exp(a_last - ac_t[0:1, :])).astype(BF16))
            c_scaled.append((c_grp[ci][g].astype(F32) * jnp.exp(ac)).astype(BF16))
        lhs_parts.append(jnp.concatenate(scores + c_scaled, axis=1))
        decays.append(jnp.concatenate([jnp.broadcast_to(jnp.tile(e, (1, GROUP_W // LANES)), (n, GROUP_W))
                                       for e in e_last], axis=0))
        incs.append(jnp.dot(jnp.concatenate(b_decayed, axis=0), xd[r], preferred_element_type=F32))

    st = st_scr[...]
    for ci, r in enumerate(rows):
        xd_c = xd[r]
        zero = jnp.zeros_like(xd_c)
        xd_heads = [jnp.where(lane_head == h, xd_c, zero) for h in range(N_HEADS)]
        rhs = jnp.concatenate(xd_heads + [st.astype(BF16)], axis=0)
        y = jnp.dot(lhs_parts[ci], rhs, preferred_element_type=F32)
        st = st * decays[ci] + jnp.where(state_block, incs[ci], 0.0)

        y = (y + xs[r] * dskip_ref[...]) * gate[r]
        for g in range(SSM_GROUPS):
            gs = slice(g * LANES, (g + 1) * LANES)
            yg = y[:, gs]
            ms = jnp.mean(yg * yg, axis=-1, keepdims=True)
            o_ref[r, gs] = (yg * lax.rsqrt(ms + EPS) * ng_ref[:, gs]).astype(BF16)
    st_scr[...] = st


def _ssd(ssd_in, ssd_bt, small, dt_rows, dtb, dtb_col, a_neg_col, seldt, d_skip, norm_g, utri_chunks, batch, seq):
    nt = seq // MIX_T
    row = lambda b, t: (b * nt + t, 0)
    return pl.pallas_call(
        _ssd_body,
        grid=(batch, nt),
        in_specs=[pl.BlockSpec((MIX_T, GROUP_W + SSM_XBC), row),
                  pl.BlockSpec((GROUP_W, MIX_T), lambda b, t: (0, b * nt + t)),
                  pl.BlockSpec((MIX_T, LANES), row),
                  pl.BlockSpec((SUBLANES, MIX_T), lambda b, t: (0, b * nt + t)),
                  _const_spec((1, LANES)),
                  _const_spec((SUBLANES, 1)),
                  _const_spec((SUBLANES, 1)),
                  _const_spec((LANES, GROUP_W)),
                  _const_spec((1, GROUP_W)),
                  _const_spec((1, GROUP_W)),
                  _const_spec((MIX_T, MIX_T))],
        out_specs=pl.BlockSpec((MIX_T, GROUP_W), row),
        out_shape=jax.ShapeDtypeStruct((batch * seq, GROUP_W), BF16),
        scratch_shapes=[pltpu.VMEM((N_HEADS * SSM_STATE, GROUP_W), F32)],
        compiler_params=_params(2),
        name="ssd",
    )(ssd_in, ssd_bt, small, dt_rows, dtb, dtb_col, a_neg_col, seldt, d_skip, norm_g, utri_chunks)


def _block_diag_ones(size, block):
    idx = np.arange(size) // block
    return jnp.asarray(idx[:, None] == idx[None, :], BF16)


def _chunk_lower_tri(size, block):
    i = np.arange(size)
    same = (i[:, None] // block) == (i[None, :] // block)
    return jnp.asarray(same & (i[None, :] <= i[:, None]), BF16)


def _lane_spread(src_lane0, width):
    m = np.zeros((LANES, N_HEADS * width))
    for h in range(N_HEADS):
        m[src_lane0 + h, h * width:(h + 1) * width] = 1.0
    return jnp.asarray(m, BF16)


def _row(v):
    return v.reshape(1, -1).astype(F32)


def _lane_slot(values, offset, width):
    return jnp.zeros((1, width), F32).at[0, offset:offset + values.shape[0]].set(values.astype(F32))


def _sublane_col(values):
    return jnp.zeros((SUBLANES, 1), F32).at[0:values.shape[0], 0].set(values.astype(F32))


def _regroup_w_in(w_in):
    gla_end = 4 * GROUP_W
    fox0 = GLA_COLS
    fox_end = fox0 + 3 * GROUP_W
    ssm0 = GLA_COLS + FOX_COLS
    ssm_end = ssm0 + GROUP_W + SSM_XBC
    sc0 = GLA_COLS + FOX_COLS + SSM_COLS
    w_main = jnp.concatenate([w_in[:, 0:gla_end], w_in[:, fox0:fox_end],
                              w_in[:, ssm0:ssm_end], w_in[:, sc0:]], axis=1).astype(BF16)
    w_small = jnp.zeros((D_MODEL, LANES), F32)
    w_small = w_small.at[:, SMALL_LR:SMALL_LR + GLA_GATE_RANK].set(w_in[:, gla_end:GLA_COLS])
    w_small = w_small.at[:, SMALL_F:SMALL_F + N_HEADS].set(w_in[:, fox_end:ssm0])
    w_small = w_small.at[:, SMALL_DT:SMALL_DT + N_HEADS].set(w_in[:, ssm_end:sc0])
    return w_main, w_small.astype(BF16)


def kernel(x, ffn1_norm, ffn1_w_gate, ffn1_w_up, ffn1_w_down, mix_norm, w_in, gla_w_gate_up, gla_b_gate, gla_norm, fox_b_forget, fox_q_norm, fox_k_norm, fox_out_norm, ssm_conv_w, ssm_conv_b, ssm_dt_bias, ssm_A_log, ssm_D, ssm_norm, sc_conv_w, sc_out_norm, w_out, ffn2_norm, ffn2_w_gate, ffn2_w_up, ffn2_w_down):
    batch, seq, _ = x.shape
    depth = w_in.shape[0]
    assert seq % MIX_T == 0 and seq % PROJ_TM == 0 and seq % ATT_TK == 0 and seq % ATT_TQ == 0
    assert (batch * seq) % FFN_TM == 0

    e64 = _block_diag_ones(GROUP_W, HEAD_DIM)
    lb_gla = _chunk_lower_tri(GLA_CHUNK, GLA_CHUNK)
    utri_t = jnp.asarray(np.triu(np.ones((PROJ_TM, PROJ_TM))), BF16)
    ub_ssd = _chunk_lower_tri(MIX_T, SSM_CHUNK).T
    seldt = _lane_spread(SMALL_DT, HEAD_DIM)

    w1g, w1u, w1d = _to_bf16(ffn1_w_gate), _to_bf16(ffn1_w_up), _to_bf16(ffn1_w_down)
    w2g, w2u, w2d = _to_bf16(ffn2_w_gate), _to_bf16(ffn2_w_up), _to_bf16(ffn2_w_down)
    wo = _to_bf16(w_out)

    xf = x.reshape(batch * seq, D_MODEL)
    for l in range(depth):
        xf = _ffn(xf, _row(ffn1_norm[l]), w1g[l], w1u[l], w1d[l])

        w_main, w_small = _regroup_w_in(w_in[l])
        gla_in, gla_vt, qn, kn, vf, f_cum, ssd_in, ssd_bt, yd, small, dt_rows = _inproj(
            xf, _row(mix_norm[l]), w_main, w_small, _sublane_col(fox_b_forget[l]),
            _row(jnp.tile(fox_q_norm[l], N_HEADS)), _row(jnp.tile(fox_k_norm[l], N_HEADS)),
            e64, utri_t, ssm_conv_w[l].astype(F32), _row(ssm_conv_b[l]),
            sc_conv_w[l].astype(F32), _row(sc_out_norm[l]), seq)

        wup_pad = jnp.zeros((LANES, GROUP_W), F32).at[SMALL_LR:SMALL_LR + GLA_GATE_RANK].set(
            gla_w_gate_up[l]).astype(BF16)
        ya = _gla(gla_in, gla_vt, small, wup_pad, _row(gla_b_gate[l]), _row(gla_norm[l]), lb_gla, e64, batch, seq)

        yb = _attn(qn, kn, vf, f_cum, _row(fox_out_norm[l]), batch, seq)

        a_neg = -jnp.exp(ssm_A_log[l].astype(F32))
        yc = _ssd(ssd_in, ssd_bt, small, dt_rows, _lane_slot(ssm_dt_bias[l], SMALL_DT, LANES),
                  _sublane_col(ssm_dt_bias[l]), _sublane_col(a_neg), seldt,
                  _row(jnp.repeat(ssm_D[l], HEAD_DIM)), _row(ssm_norm[l]), ub_ssd, batch, seq)

        xf = _ffn(xf, _row(ffn2_norm[l]), w2g[l], w2u[l], w2d[l], mixers=(ya, yb, yc, yd), w_out=wo[l])
    return xf.reshape(batch, seq, D_MODEL)
```
